```python
import math
import jax, jax.numpy as jnp
from jax import lax
import numpy as np

D_MODEL = 4096
BATCH = 8
SEQ = 2048
DEPTH = 4

CHUNK = 64
Q_BLOCK = 128
EPS = 1e-6
N_MOD = 6
N_EVEN = (DEPTH + 1) // 2
N_ODD = DEPTH // 2
FFN_HIDDEN = 4 * D_MODEL

SSD_WIDTH = D_MODEL // 2
SSD_HEAD_DIM = 64
SSD_HEADS = SSD_WIDTH // SSD_HEAD_DIM
SSD_GROUPS = 8
SSD_HEADS_PER_GROUP = SSD_HEADS // SSD_GROUPS
SSD_STATE = 128
SSD_CONV = 4
SSD_CONV_DIM = SSD_WIDTH + 2 * SSD_GROUPS * SSD_STATE

MLA_V_DIM = 128
MLA_HEADS = (D_MODEL - SSD_WIDTH) // MLA_V_DIM
MLA_NOPE = 128
MLA_ROPE = 64
MLA_Q_RANK = D_MODEL // 4
MLA_KV_RANK = D_MODEL // 8
ROPE_THETA = 10000.0

IN_SPLITS = (SSD_WIDTH, SSD_CONV_DIM, SSD_HEADS, MLA_Q_RANK, MLA_KV_RANK, MLA_ROPE)
E_IN_WIDTH = sum(IN_SPLITS)
MIX_WIDTH = SSD_WIDTH + MLA_HEADS * MLA_V_DIM

LRU_BLOCK = 256
LRU_WIDTH = (5 * D_MODEL // 4) // LRU_BLOCK * LRU_BLOCK
LRU_BLOCKS = LRU_WIDTH // LRU_BLOCK
LRU_CONV = 4
LRU_C = 8.0

kernel_name = "hybrid_ssd_mla_rglru_adaln_trunk"


def _split(x, sizes):
    return jnp.split(x, np.cumsum(sizes)[:-1].tolist(), axis=-1)


def rmsnorm(x, gain):
    xf = x.astype(jnp.float32)
    y = xf * lax.rsqrt(jnp.mean(xf * xf, axis=-1, keepdims=True) + EPS)
    return (y * gain.astype(jnp.float32)).astype(x.dtype)


def group_rmsnorm(y, gain, groups):
    b, l, w = y.shape
    yg = y.reshape(b, l, groups, w // groups)
    yg = yg * lax.rsqrt(jnp.mean(yg * yg, axis=-1, keepdims=True) + EPS)
    return yg.reshape(b, l, w) * gain.astype(jnp.float32)


def causal_depthwise_conv(x, w, bias):
    k = w.shape[0]
    y = lax.conv_general_dilated(
        x, w[:, None, :].astype(x.dtype), window_strides=(1,), padding=[(k - 1, 0)],
        dimension_numbers=('NWC', 'WIO', 'NWC'), feature_group_count=x.shape[-1])
    return y + bias


def rope(x, cos, sin):
    x1, x2 = jnp.split(x, 2, axis=-1)
    return jnp.concatenate([x1 * cos - x2 * sin, x2 * cos + x1 * sin], axis=-1)


def segsum(a):
    cs = jnp.cumsum(a, axis=-1)
    diff = cs[..., :, None] - cs[..., None, :]
    t = a.shape[-1]
    mask = jnp.tril(jnp.ones((t, t), dtype=bool))
    return jnp.where(mask, diff, -jnp.inf)


def ssd_scan(x, dt, a, bmat, cmat):
    bsz, seq = x.shape[0], x.shape[1]
    nc = seq // CHUNK
    g, r, p, n = SSD_GROUPS, SSD_HEADS_PER_GROUP, SSD_HEAD_DIM, SSD_STATE
    xs = (x * dt[..., None]).reshape(bsz, nc, CHUNK, g, r, p)
    ad = (dt * a).reshape(bsz, nc, CHUNK, g, r)
    ad = jnp.transpose(ad, (0, 1, 3, 4, 2))
    bc = bmat.reshape(bsz, nc, CHUNK, g, n)
    cc = cmat.reshape(bsz, nc, CHUNK, g, n)
    a_cs = jnp.cumsum(ad, axis=-1)
    decay_in = jnp.exp(segsum(ad))
    cb = jnp.einsum('bclgn,bcsgn->bcgls', cc, bc)
    y_diag = jnp.einsum('bcgls,bcgrls,bcsgrp->bclgrp', cb, decay_in, xs)
    decay_to_end = jnp.exp(a_cs[..., -1:] - a_cs)
    chunk_states = jnp.einsum('bclgn,bcgrl,bclgrp->bcgrpn', bc, decay_to_end, xs)
    chunk_decay = jnp.exp(a_cs[..., -1])

    def step(state, inp):
        dec, new = inp
        return dec[..., None, None] * state + new, state

    init = jnp.zeros((bsz, g, r, p, n), dtype=xs.dtype)
    _, prev = lax.scan(step, init, (jnp.moveaxis(chunk_decay, 1, 0),
                                    jnp.moveaxis(chunk_states, 1, 0)))
    prev = jnp.moveaxis(prev, 0, 1)
    y_off = jnp.einsum('bclgn,bcgrpn,bcgrl->bclgrp', cc, prev, jnp.exp(a_cs))
    return (y_diag + y_off).reshape(bsz, seq, g * r, p)


def chunk_causal_mla(q_nope, q_rope, k_nope, k_rope, v):
    seq = q_nope.shape[1]
    scale = (MLA_NOPE + MLA_ROPE) ** -0.5
    outs = []
    for i in range(seq // Q_BLOCK):
        q0, q1 = i * Q_BLOCK, (i + 1) * Q_BLOCK
        s = (jnp.einsum('bqhd,bkhd->bhqk', q_nope[:, q0:q1], k_nope[:, :q1])
             + jnp.einsum('bqhr,bkr->bhqk', q_rope[:, q0:q1], k_rope[:, :q1]))
        s = s.astype(jnp.float32) * scale
        q_chunk = (q0 + jnp.arange(Q_BLOCK)) // CHUNK
        k_chunk = jnp.arange(q1) // CHUNK
        s = jnp.where(k_chunk[None, :] <= q_chunk[:, None], s, -jnp.inf)
        prob = jax.nn.softmax(s, axis=-1).astype(v.dtype)
        outs.append(jnp.einsum('bhqk,bkhd->bqhd', prob, v[:, :q1]))
    return jnp.concatenate(outs, axis=1)


def ssd_mla_mixer(h, cos, sin, w_in, conv_w, conv_b, dt_bias, a_log, d_skip, ssd_norm,
                  q_norm, w_uq, kv_norm, w_ukv, w_out):
    bsz, seq, _ = h.shape
    f32 = jnp.float32
    z, xbc, dt, cq, ckv, kr = _split(h @ w_in, IN_SPLITS)
    xbc = jax.nn.silu(causal_depthwise_conv(xbc, conv_w, conv_b))
    xs, bs, cs = _split(xbc, (SSD_WIDTH, SSD_GROUPS * SSD_STATE, SSD_GROUPS * SSD_STATE))
    xs = xs.reshape(bsz, seq, SSD_HEADS, SSD_HEAD_DIM).astype(f32)
    dt = jax.nn.softplus(dt.astype(f32) + dt_bias.astype(f32))
    a = -jnp.exp(a_log.astype(f32))
    y = ssd_scan(xs, dt, a,
                 bs.reshape(bsz, seq, SSD_GROUPS, SSD_STATE).astype(f32),
                 cs.reshape(bsz, seq, SSD_GROUPS, SSD_STATE).astype(f32))
    y = y + d_skip.astype(f32)[:, None] * xs
    y = y.reshape(bsz, seq, SSD_WIDTH) * jax.nn.silu(z.astype(f32))
    y_ssd = group_rmsnorm(y, ssd_norm, SSD_GROUPS).astype(h.dtype)
    q = (rmsnorm(cq, q_norm) @ w_uq).reshape(bsz, seq, MLA_HEADS, MLA_NOPE + MLA_ROPE)
    q_nope = q[..., :MLA_NOPE]
    q_rope = rope(q[..., MLA_NOPE:], cos[:, :, None, :], sin[:, :, None, :])
    kv = (rmsnorm(ckv, kv_norm) @ w_ukv).reshape(bsz, seq, MLA_HEADS, MLA_NOPE + MLA_V_DIM)
    k_nope, v = kv[..., :MLA_NOPE], kv[..., MLA_NOPE:]
    k_rope = rope(kr, cos, sin)
    o = chunk_causal_mla(q_nope, q_rope, k_nope, k_rope, v).reshape(bsz, seq, MLA_HEADS * MLA_V_DIM)
    return jnp.concatenate([y_ssd, o], axis=-1) @ w_out


def rglru_mixer(h, w_y, b_y, w_x, b_x, conv_w, conv_b, w_a, b_a, w_i, b_i, lam, w_out, b_out):
    bsz, seq, _ = h.shape
    f32 = jnp.float32
    gate = jax.nn.gelu(h @ w_y + b_y)
    u = causal_depthwise_conv(h @ w_x + b_x, conv_w, conv_b)
    ub = u.reshape(bsz, seq, LRU_BLOCKS, LRU_BLOCK)
    r_gate = jax.nn.sigmoid((jnp.einsum('blnd,nde->blne', ub, w_a).reshape(bsz, seq, LRU_WIDTH)
                             + b_a).astype(f32))
    i_gate = jax.nn.sigmoid((jnp.einsum('blnd,nde->blne', ub, w_i).reshape(bsz, seq, LRU_WIDTH)
                             + b_i).astype(f32))
    log_a = -LRU_C * r_gate * jax.nn.softplus(-lam.astype(f32))
    a = jnp.exp(log_a)
    bx = jnp.sqrt(-jnp.expm1(2.0 * log_a)) * (i_gate * u.astype(f32))

    def step(hc, inp):
        a_t, b_t = inp
        hn = a_t * hc + b_t
        return hn, hn

    _, hs = lax.scan(step, jnp.zeros((bsz, LRU_WIDTH), f32),
                     (jnp.swapaxes(a, 0, 1), jnp.swapaxes(bx, 0, 1)))
    y = jnp.swapaxes(hs, 0, 1).astype(h.dtype) * gate
    return y @ w_out + b_out


def setup_inputs(seed: int = 0) -> dict:
    key = jax.random.key(seed)
    ks = iter(jax.random.split(key, 48))
    f32 = jnp.float32

    def nrm(shape, scale):
        return jax.random.normal(next(ks), shape, f32) * scale

    def gain(shape):
        return 1.0 + nrm(shape, 0.05)

    x = nrm((BATCH, SEQ, D_MODEL), 1.0)
    c = nrm((BATCH, D_MODEL), 1.0)
    offset = jax.random.randint(next(ks), (BATCH, 1), 0, 4096, dtype=jnp.int32)
    positions = offset + jnp.arange(SEQ, dtype=jnp.int32)[None, :]

    ada_w = nrm((D_MODEL, N_MOD * D_MODEL), 0.5 * D_MODEL ** -0.5)
    ada_b = nrm((N_MOD * D_MODEL,), 0.02)
    ada_table = nrm((DEPTH, N_MOD, D_MODEL), 0.1)
    norm_mix = gain((DEPTH, D_MODEL))
    norm_ffn = gain((DEPTH, D_MODEL))
    norm_final = gain((D_MODEL,))
    ffn_up = nrm((DEPTH, D_MODEL, FFN_HIDDEN), D_MODEL ** -0.5)
    ffn_down = nrm((DEPTH, FFN_HIDDEN, D_MODEL), FFN_HIDDEN ** -0.5)

    e_w_in = nrm((N_EVEN, D_MODEL, E_IN_WIDTH), D_MODEL ** -0.5)
    e_conv_w = nrm((N_EVEN, SSD_CONV, SSD_CONV_DIM), SSD_CONV ** -0.5)
    e_conv_b = nrm((N_EVEN, SSD_CONV_DIM), 0.02)
    dt0 = jnp.exp(jax.random.uniform(next(ks), (N_EVEN, SSD_HEADS), f32,
                                     math.log(1e-3), math.log(1e-1)))
    e_dt_bias = dt0 + jnp.log(-jnp.expm1(-dt0))
    e_a_log = jnp.log(jax.random.uniform(next(ks), (N_EVEN, SSD_HEADS), f32, 1.0, 16.0))
    e_d_skip = gain((N_EVEN, SSD_HEADS))
    e_ssd_norm = gain((N_EVEN, SSD_WIDTH))
    e_q_norm = gain((N_EVEN, MLA_Q_RANK))
    e_w_uq = nrm((N_EVEN, MLA_Q_RANK, MLA_HEADS * (MLA_NOPE + MLA_ROPE)), MLA_Q_RANK ** -0.5)
    e_kv_norm = gain((N_EVEN, MLA_KV_RANK))
    e_w_ukv = nrm((N_EVEN, MLA_KV_RANK, MLA_HEADS * (MLA_NOPE + MLA_V_DIM)), MLA_KV_RANK ** -0.5)
    e_w_out = nrm((N_EVEN, MIX_WIDTH, D_MODEL), MIX_WIDTH ** -0.5)

    o_w_y = nrm((N_ODD, D_MODEL, LRU_WIDTH), D_MODEL ** -0.5)
    o_b_y = nrm((N_ODD, LRU_WIDTH), 0.02)
    o_w_x = nrm((N_ODD, D_MODEL, LRU_WIDTH), D_MODEL ** -0.5)
    o_b_x = nrm((N_ODD, LRU_WIDTH), 0.02)
    o_conv_w = nrm((N_ODD, LRU_CONV, LRU_WIDTH), LRU_CONV ** -0.5)
    o_conv_b = nrm((N_ODD, LRU_WIDTH), 0.02)
    o_w_a = nrm((N_ODD, LRU_BLOCKS, LRU_BLOCK, LRU_BLOCK), LRU_BLOCK ** -0.5)
    o_b_a = nrm((N_ODD, LRU_WIDTH), 0.02)
    o_w_i = nrm((N_ODD, LRU_BLOCKS, LRU_BLOCK, LRU_BLOCK), LRU_BLOCK ** -0.5)
    o_b_i = nrm((N_ODD, LRU_WIDTH), 0.02)
    a8 = jax.random.uniform(next(ks), (N_ODD, LRU_WIDTH), f32, 0.9, 0.999)
    s0 = a8 ** (1.0 / LRU_C)
    o_lam = jnp.log(s0) - jnp.log1p(-s0)
    o_w_out = nrm((N_ODD, LRU_WIDTH, D_MODEL), LRU_WIDTH ** -0.5)
    o_b_out = nrm((N_ODD, D_MODEL), 0.02)

    return {
        "x": x, "c": c, "positions": positions,
        "ada_w": ada_w, "ada_b": ada_b, "ada_table": ada_table,
        "norm_mix": norm_mix, "norm_ffn": norm_ffn, "norm_final": norm_final,
        "ffn_up": ffn_up, "ffn_down": ffn_down,
        "e_w_in": e_w_in, "e_conv_w": e_conv_w, "e_conv_b": e_conv_b,
        "e_dt_bias": e_dt_bias, "e_a_log": e_a_log, "e_d_skip": e_d_skip,
        "e_ssd_norm": e_ssd_norm, "e_q_norm": e_q_norm, "e_w_uq": e_w_uq,
        "e_kv_norm": e_kv_norm, "e_w_ukv": e_w_ukv, "e_w_out": e_w_out,
        "o_w_y": o_w_y, "o_b_y": o_b_y, "o_w_x": o_w_x, "o_b_x": o_b_x,
        "o_conv_w": o_conv_w, "o_conv_b": o_conv_b, "o_w_a": o_w_a, "o_b_a": o_b_a,
        "o_w_i": o_w_i, "o_b_i": o_b_i, "o_lam": o_lam, "o_w_out": o_w_out, "o_b_out": o_b_out,
    }


def reference(x, c, positions, ada_w, ada_b, ada_table, norm_mix, norm_ffn, norm_final,
              ffn_up, ffn_down,
              e_w_in, e_conv_w, e_conv_b, e_dt_bias, e_a_log, e_d_skip, e_ssd_norm,
              e_q_norm, e_w_uq, e_kv_norm, e_w_ukv, e_w_out,
              o_w_y, o_b_y, o_w_x, o_b_x, o_conv_w, o_conv_b, o_w_a, o_b_a,
              o_w_i, o_b_i, o_lam, o_w_out, o_b_out):
    f32 = jnp.float32
    inv_freq = ROPE_THETA ** (-jnp.arange(0, MLA_ROPE, 2, dtype=f32) / MLA_ROPE)
    ang = positions.astype(f32)[..., None] * inv_freq
    cos, sin = jnp.cos(ang).astype(x.dtype), jnp.sin(ang).astype(x.dtype)
    mod_all = (jax.nn.silu(c) @ ada_w + ada_b).reshape(c.shape[0], N_MOD, D_MODEL)
    for layer in range(DEPTH):
        mod = (mod_all + ada_table[layer])[:, :, None, :]
        shift_m, scale_m, gate_m = mod[:, 0], mod[:, 1], mod[:, 2]
        shift_f, scale_f, gate_f = mod[:, 3], mod[:, 4], mod[:, 5]
        hm = rmsnorm(x, norm_mix[layer]) * (1.0 + scale_m) + shift_m
        j = layer // 2
        if layer % 2 == 0:
            mix = ssd_mla_mixer(hm, cos, sin, e_w_in[j], e_conv_w[j], e_conv_b[j], e_dt_bias[j],
                                e_a_log[j], e_d_skip[j], e_ssd_norm[j], e_q_norm[j], e_w_uq[j],
                                e_kv_norm[j], e_w_ukv[j], e_w_out[j])
        else:
            mix = rglru_mixer(hm, o_w_y[j], o_b_y[j], o_w_x[j], o_b_x[j], o_conv_w[j], o_conv_b[j],
                              o_w_a[j], o_b_a[j], o_w_i[j], o_b_i[j], o_lam[j], o_w_out[j], o_b_out[j])
        x = x + gate_m * mix
        hf = rmsnorm(x, norm_ffn[layer]) * (1.0 + scale_f) + shift_f
        x = x + gate_f * (jnp.square(jax.nn.relu(hf @ ffn_up[layer])) @ ffn_down[layer])
    return rmsnorm(x, norm_final)
```

```python
import functools

import jax
import jax.numpy as jnp
from jax import lax
from jax.experimental import pallas as pl
from jax.experimental.pallas import tpu as pltpu

F32 = jnp.float32
BF16 = jnp.bfloat16

EPS = 1e-6
ROPE_THETA = 10000.0
N_MOD = 6

SSD_HEAD_DIM = 64
SSD_GROUPS = 8
SSD_HEADS_PER_GROUP = 4
SSD_STATE = 128
SSD_CONV = 4
SSD_HEADS = SSD_GROUPS * SSD_HEADS_PER_GROUP
SSD_WIDTH = SSD_HEADS * SSD_HEAD_DIM
SSD_BC_WIDTH = SSD_GROUPS * SSD_STATE
SSD_TILE = 256

MLA_HEADS = 16
MLA_NOPE = 128
MLA_ROPE = 64
MLA_V_DIM = 128
MLA_STREAM_CHUNK = 64
MLA_Q_TILE = 256

LRU_BLOCK = 256
LRU_CONV = 4
LRU_C = 8.0
LRU_TILE = 256
LRU_COLS = 512

LANES = 128
SUBLANES = 8
DT_LANE0 = 64
VMEM_LIMIT_CAP = 60000 * 1024


def _vmem_limit(block_bytes, extra=0):
    need = 2 * block_bytes + extra + (4 << 20)
    return int(min(max(need, 16 << 20), VMEM_LIMIT_CAP))


def _params(semantics, vmem):
    return pltpu.CompilerParams(dimension_semantics=semantics, vmem_limit_bytes=vmem)


def _nbytes(shape, dtype):
    n = 1
    for s in shape:
        n *= s
    return n * jnp.dtype(dtype).itemsize


def _silu(x):
    return x * jax.nn.sigmoid(x)


def _softplus(x):
    return jnp.maximum(x, 0.0) + jnp.log(1.0 + jnp.exp(-jnp.abs(x)))


def _split3(v):
    a1 = v.astype(BF16)
    r1 = v - a1.astype(F32)
    a2 = r1.astype(BF16)
    a3 = (r1 - a2.astype(F32)).astype(BF16)
    return a1, a2, a3


def _dot(a, b):
    return jnp.dot(a, b, preferred_element_type=F32)


def _dot_nt(a, b):
    return lax.dot_general(a, b, (((1,), (1,)), ((), ())), preferred_element_type=F32)


def _dot_tn(a, b):
    return lax.dot_general(a, b, (((0,), (0,)), ((), ())), preferred_element_type=F32)


def _select_dot(sel, v):
    a1, a2, a3 = _split3(v)
    return _dot(sel, a1) + _dot(sel, a2) + _dot(sel, a3)


def _dot_select(v, sel):
    a1, a2, a3 = _split3(v)
    return _dot(a1, sel) + _dot(a2, sel) + _dot(a3, sel)


def _ada_kernel(c_ref, w_ref, b_ref, o_ref):
    s = _silu(c_ref[...]).astype(BF16)
    o_ref[...] = _dot(s, w_ref[...].astype(BF16)) + b_ref[...]


def _ada_proj(c, w, b):
    bsz, d = c.shape
    n = w.shape[1]
    tn = 512
    blocks = _nbytes((d, tn), F32) + _nbytes((bsz, d), F32) + 2 * _nbytes((bsz, tn), F32)
    return pl.pallas_call(
        _ada_kernel,
        out_shape=jax.ShapeDtypeStruct((bsz, n), F32),
        grid=(n // tn,),
        in_specs=[pl.BlockSpec((bsz, d), lambda j: (0, 0)),
                  pl.BlockSpec((d, tn), lambda j: (0, j)),
                  pl.BlockSpec((1, tn), lambda j: (0, j))],
        out_specs=pl.BlockSpec((bsz, tn), lambda j: (0, j)),
        compiler_params=_params(("arbitrary",), _vmem_limit(blocks, _nbytes((d, tn), BF16))),
        name="ada_proj",
    )(c, w, b.reshape(1, n))


def _rope_table_kernel(pos_ref, freq_ref, cc_ref, ss_ref):
    ang = pos_ref[...].astype(F32) * freq_ref[...]
    lane = lax.broadcasted_iota(jnp.int32, ang.shape, 1)
    half = MLA_ROPE // 2
    c = jnp.cos(ang)
    s = jnp.sin(ang)
    cc_ref[...] = jnp.where(lane < MLA_ROPE, c, 0.0)
    ss_ref[...] = jnp.where(lane < half, -s, jnp.where(lane < MLA_ROPE, s, 0.0))


def _rope_tables(positions):
    t = positions.size
    half = MLA_ROPE // 2
    inv_freq = ROPE_THETA ** (-jnp.arange(0, MLA_ROPE, 2, dtype=F32) / MLA_ROPE)
    freq = jnp.concatenate([inv_freq, inv_freq, jnp.zeros((LANES - 2 * half,), F32)]).reshape(1, LANES)
    tm = min(t, 2048)
    return pl.pallas_call(
        _rope_table_kernel,
        out_shape=(jax.ShapeDtypeStruct((t, LANES), F32), jax.ShapeDtypeStruct((t, LANES), F32)),
        grid=(t // tm,),
        in_specs=[pl.BlockSpec((tm, 1), lambda i: (i, 0)),
                  pl.BlockSpec((1, LANES), lambda i: (0, 0))],
        out_specs=(pl.BlockSpec((tm, LANES), lambda i: (i, 0)),
                   pl.BlockSpec((tm, LANES), lambda i: (i, 0))),
        compiler_params=_params(("arbitrary",), _vmem_limit(4 * _nbytes((tm, LANES), F32))),
        name="rope_tables",
    )(positions.reshape(t, 1), freq)


def _rownorm_kernel(x_ref, g_ref, *rest, modulate):
    o_ref = rest[-1]
    x = x_ref[...]
    y = x * lax.rsqrt(jnp.mean(x * x, axis=-1, keepdims=True) + EPS) * g_ref[...]
    if modulate:
        sc_ref, sh_ref = rest[0], rest[1]
        y = y * (1.0 + sc_ref[...]) + sh_ref[...]
    o_ref[...] = y.astype(o_ref.dtype)


def _rownorm(x, gain, *, col_block=0, width=None, scale=None, shift=None, rows_per_batch=None,
             out_dtype=BF16):
    t = x.shape[0]
    width = x.shape[1] if width is None else width
    tm = min(512, t)
    modulate = scale is not None
    in_specs = [pl.BlockSpec((tm, width), lambda i: (i, col_block)),
                pl.BlockSpec((1, width), lambda i: (0, 0))]
    args = [x, gain.reshape(1, width)]
    if modulate:
        bmap = lambda i: ((i * tm) // rows_per_batch, 0, 0)
        in_specs += [pl.BlockSpec((None, 1, width), bmap), pl.BlockSpec((None, 1, width), bmap)]
        args += [scale, shift]
    blocks = _nbytes((tm, width), F32) + _nbytes((tm, width), out_dtype) + 3 * _nbytes((1, width), F32)
    return pl.pallas_call(
        functools.partial(_rownorm_kernel, modulate=modulate),
        out_shape=jax.ShapeDtypeStruct((t, width), out_dtype),
        grid=(t // tm,),
        in_specs=in_specs,
        out_specs=pl.BlockSpec((tm, width), lambda i: (i, 0)),
        compiler_params=_params(("arbitrary",), _vmem_limit(blocks, 2 * _nbytes((tm, width), F32))),
        name="rownorm",
    )(*args)


def _mm_kernel(x_ref, w_ref, *rest, act, has_bias):
    o_ref = rest[-1]
    acc = _dot(x_ref[...], w_ref[...])
    if has_bias:
        acc = acc + rest[0][...]
    if act == "relu2":
        r = jnp.maximum(acc, 0.0)
        acc = r * r
    o_ref[...] = acc.astype(o_ref.dtype)


def _mm(x, w, *, bias=None, act=None, out_dtype=F32, tm=1024, tn=1024):
    m, k = x.shape
    n = w.shape[1]
    tm = min(tm, m)
    tn = min(tn, n)
    has_bias = bias is not None
    in_specs = [pl.BlockSpec((tm, k), lambda i, j: (i, 0)),
                pl.BlockSpec((k, tn), lambda i, j: (0, j))]
    args = [x, w]
    if has_bias:
        in_specs.append(pl.BlockSpec((1, tn), lambda i, j: (0, j)))
        args.append(bias.reshape(1, n))
    blocks = _nbytes((tm, k), x.dtype) + _nbytes((k, tn), w.dtype) + _nbytes((tm, tn), out_dtype)
    return pl.pallas_call(
        functools.partial(_mm_kernel, act=act, has_bias=has_bias),
        out_shape=jax.ShapeDtypeStruct((m, n), out_dtype),
        grid=(m // tm, n // tn),
        in_specs=in_specs,
        out_specs=pl.BlockSpec((tm, tn), lambda i, j: (i, j)),
        compiler_params=_params(("arbitrary", "arbitrary"),
                                _vmem_limit(blocks, 3 * _nbytes((tm, tn), F32))),
        name="mm",
    )(*args)


def _mm_res_kernel(*refs, n_pairs, has_bias, nk):
    xs = refs[0:2 * n_pairs:2]
    ws = refs[1:2 * n_pairs:2]
    pos = 2 * n_pairs
    res_ref, gate_ref = refs[pos], refs[pos + 1]
    pos += 2
    bias_ref = refs[pos] if has_bias else None
    pos += int(has_bias)
    o_ref = refs[pos]
    acc_ref = refs[pos + 1] if nk > 1 else None

    def partial_sum():
        acc = _dot(xs[0][...], ws[0][...])
        for x_ref, w_ref in zip(xs[1:], ws[1:]):
            acc = acc + _dot(x_ref[...], w_ref[...])
        return acc

    def finish(acc):
        if has_bias:
            acc = acc + bias_ref[...]
        o_ref[...] = res_ref[...] + gate_ref[...] * acc

    if nk == 1:
        finish(partial_sum())
        return

    kk = pl.program_id(2)

    @pl.when(kk == 0)
    def _():
        acc_ref[...] = partial_sum()

    @pl.when(kk > 0)
    def _():
        acc_ref[...] += partial_sum()

    @pl.when(kk == nk - 1)
    def _():
        finish(acc_ref[...])


def _mm_res(pairs, res, gate, *, bias=None, rows_per_batch, tm=1024, tn=1024, tk=None):
    m, n = res.shape
    k = pairs[0][0].shape[1]
    tk = k if tk is None else tk
    nk = k // tk
    tm = min(tm, m)
    has_bias = bias is not None
    in_specs, args = [], []
    blocks = 0
    for x, w in pairs:
        in_specs += [pl.BlockSpec((tm, tk), lambda i, j, kk: (i, kk)),
                     pl.BlockSpec((tk, tn), lambda i, j, kk: (kk, j))]
        args += [x, w]
        blocks += _nbytes((tm, tk), x.dtype) + _nbytes((tk, tn), w.dtype)
    in_specs += [pl.BlockSpec((tm, tn), lambda i, j, kk: (i, j)),
                 pl.BlockSpec((None, 1, tn), lambda i, j, kk: ((i * tm) // rows_per_batch, 0, j))]
    args += [res, gate]
    blocks += 2 * _nbytes((tm, tn), F32)
    if has_bias:
        in_specs.append(pl.BlockSpec((1, tn), lambda i, j, kk: (0, j)))
        args.append(bias.reshape(1, n))
    scratch = [pltpu.VMEM((tm, tn), F32)] if nk > 1 else []
    return pl.pallas_call(
        functools.partial(_mm_res_kernel, n_pairs=len(pairs), has_bias=has_bias, nk=nk),
        out_shape=jax.ShapeDtypeStruct((m, n), F32),
        grid=(m // tm, n // tn, nk),
        in_specs=in_specs,
        out_specs=pl.BlockSpec((tm, tn), lambda i, j, kk: (i, j)),
        scratch_shapes=scratch,
        compiler_params=_params(("arbitrary", "arbitrary", "arbitrary"),
                                _vmem_limit(blocks, 4 * _nbytes((tm, tn), F32))),
        name="mm_res",
    )(*args)


def _attn_kernel(qn_ref, qr_ref, kn_ref, v_ref, kr_ref, cc_ref, ss_ref, o_ref, *, scale, tq):
    seq = qn_ref.shape[0]
    cc = cc_ref[...]
    ss = ss_ref[...]
    lane = lax.broadcasted_iota(jnp.int32, (seq, LANES), 1)
    half = MLA_ROPE // 2

    def rope(x):
        swapped = jnp.where(lane < half, pltpu.roll(x, LANES - half, 1), pltpu.roll(x, half, 1))
        return x * cc + swapped * ss

    qn = (qn_ref[...] * scale).astype(BF16)
    qr = (rope(qr_ref[...]) * scale).astype(BF16)
    kn = kn_ref[...].astype(BF16)
    kr = rope(kr_ref[...]).astype(BF16)
    v = v_ref[...].astype(BF16)

    r_chunk = lax.broadcasted_iota(jnp.int32, (tq, tq), 0) // MLA_STREAM_CHUNK
    c_chunk = lax.broadcasted_iota(jnp.int32, (tq, tq), 1) // MLA_STREAM_CHUNK
    diag_ok = c_chunk <= r_chunk

    for i in range(seq // tq):
        q0, q1 = i * tq, (i + 1) * tq
        s_d = _dot_nt(qn[q0:q1], kn[q0:q1]) + _dot_nt(qr[q0:q1], kr[q0:q1])
        s_d = jnp.where(diag_ok, s_d, -jnp.inf)
        m = jnp.max(s_d, axis=-1, keepdims=True)
        if i > 0:
            s_o = _dot_nt(qn[q0:q1], kn[:q0]) + _dot_nt(qr[q0:q1], kr[:q0])
            m = jnp.maximum(m, jnp.max(s_o, axis=-1, keepdims=True))
        p_d = jnp.exp(s_d - m)
        denom = jnp.sum(p_d, axis=-1, keepdims=True)
        acc = _dot(p_d.astype(BF16), v[q0:q1])
        if i > 0:
            p_o = jnp.exp(s_o - m)
            denom = denom + jnp.sum(p_o, axis=-1, keepdims=True)
            acc = acc + _dot(p_o.astype(BF16), v[:q0])
        o_ref[q0:q1, :] = (acc / denom).astype(o_ref.dtype)


def _attention(q, kv, krdt, cc, ss, *, bsz, seq):
    t = q.shape[0]
    scale = (MLA_NOPE + MLA_ROPE) ** -0.5
    blk = lambda off: pl.BlockSpec((seq, LANES), lambda b, h: (b, off + h))
    tok = pl.BlockSpec((seq, LANES), lambda b, h: (b, 0))
    blocks = 7 * _nbytes((seq, LANES), F32) + _nbytes((seq, LANES), BF16)
    return pl.pallas_call(
        functools.partial(_attn_kernel, scale=scale, tq=MLA_Q_TILE),
        out_shape=jax.ShapeDtypeStruct((t, MLA_HEADS * MLA_V_DIM), BF16),
        grid=(bsz, MLA_HEADS),
        in_specs=[blk(0), blk(MLA_HEADS), blk(0), blk(MLA_HEADS), tok, tok, tok],
        out_specs=pl.BlockSpec((seq, MLA_V_DIM), lambda b, h: (b, h)),
        compiler_params=_params(("arbitrary", "arbitrary"), _vmem_limit(blocks, 16 << 20)),
        name="mla_attention",
    )(q, q, kv, kv, krdt, cc, ss)


def _ssd_kernel(z_ref, x_ref, bc_ref, dk_ref, cwx_ref, cbx_ref, cwbc_ref, cbbc_ref,
                dtb_ref, alog_ref, dskip_ref, gain_ref, o_ref,
                tailx_ref, tailbc_ref, state_ref):
    q = x_ref.shape[0]
    p = SSD_HEAD_DIM
    gw = SSD_HEADS_PER_GROUP * p
    n = SSD_STATE

    @pl.when(pl.program_id(1) == 0)
    def _():
        tailx_ref[...] = jnp.zeros_like(tailx_ref)
        tailbc_ref[...] = jnp.zeros_like(tailbc_ref)
        state_ref[...] = jnp.zeros_like(state_ref)

    def conv_silu(tail_ref, blk_ref, w_ref, b_ref):
        blk = blk_ref[...]
        xp = jnp.concatenate([tail_ref[...], blk], axis=0)
        w = w_ref[...]
        y = b_ref[...]
        for kk in range(SSD_CONV):
            off = SUBLANES - (SSD_CONV - 1) + kk
            y = y + w[kk:kk + 1] * xp[off:off + q]
        tail_ref[...] = blk[q - SUBLANES:]
        return _silu(y)

    xc = conv_silu(tailx_ref, x_ref, cwx_ref, cbx_ref)
    bcc = conv_silu(tailbc_ref, bc_ref, cwbc_ref, cbbc_ref)

    lane = lax.broadcasted_iota(jnp.int32, (q, LANES), 1)
    lane1 = lax.broadcasted_iota(jnp.int32, (1, LANES), 1)
    is_dt = (lane1 >= DT_LANE0) & (lane1 < DT_LANE0 + SSD_HEADS)
    dt = _softplus(dk_ref[...] + dtb_ref[...])
    a_row = jnp.where(is_dt, -jnp.exp(alog_ref[...]), 0.0)
    ad = dt * a_row

    rr = lax.broadcasted_iota(jnp.int32, (q, q), 0)
    cc_ = lax.broadcasted_iota(jnp.int32, (q, q), 1)
    causal = cc_ <= rr
    tri = jnp.where(causal, 1.0, 0.0).astype(BF16)
    a_cs = _select_dot(tri, ad)
    a_cs_t = a_cs.T

    er = lax.broadcasted_iota(jnp.int32, (LANES, SSD_WIDTH), 0)
    ec = lax.broadcasted_iota(jnp.int32, (LANES, SSD_WIDTH), 1)
    expand = jnp.where(er - DT_LANE0 == ec // p, 1.0, 0.0).astype(BF16)
    dt_x = _dot_select(dt, expand)
    acs_x = _dot_select(a_cs, expand)
    dec_x = jnp.exp(acs_x)
    aend_x = acs_x[q - 1:q, :]
    xs = xc * dt_x
    xs_end = xs * jnp.exp(aend_x - acs_x)
    chunk_decay = dec_x[q - 1:q, :]

    lo_half = lane < p
    zs = _silu(z_ref[...])
    dskip = dskip_ref[...]
    gain = gain_ref[...]

    for g in range(SSD_GROUPS):
        gs = slice(g * gw, (g + 1) * gw)
        b_g = bcc[:, g * n:(g + 1) * n].astype(BF16)
        c_g = bcc[:, SSD_BC_WIDTH + g * n:SSD_BC_WIDTH + (g + 1) * n].astype(BF16)
        cb = _dot_nt(c_g, b_g)
        s_prev = state_ref[g]
        y_off = _dot(c_g, s_prev.astype(BF16)) * dec_x[:, gs]
        s_new = _dot_tn(b_g, xs_end[:, gs].astype(BF16))
        state_ref[g] = chunk_decay[:, gs] * s_prev + s_new

        pair_out = []
        for pr in range(SSD_HEADS_PER_GROUP // 2):
            ps = slice(g * gw + pr * LANES, g * gw + (pr + 1) * LANES)
            xs_pair = xs[:, ps]
            acc = None
            for hh in range(2):
                h = g * SSD_HEADS_PER_GROUP + pr * 2 + hh
                col = jnp.sum(jnp.where(lane == DT_LANE0 + h, a_cs, 0.0), axis=-1, keepdims=True)
                row = a_cs_t[DT_LANE0 + h:DT_LANE0 + h + 1, :]
                seg = jnp.exp(jnp.where(causal, col - row, -jnp.inf))
                mh = (cb * seg).astype(BF16)
                keep = lo_half if hh == 0 else jnp.logical_not(lo_half)
                term = _dot(mh, jnp.where(keep, xs_pair, 0.0).astype(BF16))
                acc = term if acc is None else acc + term
            pair_out.append(acc)
        y = jnp.concatenate(pair_out, axis=-1) + y_off + dskip[:, gs] * xc[:, gs]
        y = y * zs[:, gs]
        y = y * lax.rsqrt(jnp.mean(y * y, axis=-1, keepdims=True) + EPS) * gain[:, gs]
        o_ref[:, gs] = y.astype(o_ref.dtype)


def _ssd(main, krdt, conv_w, conv_b, dt_bias, a_log, d_skip, ssd_norm, *, bsz, seq):
    t = main.shape[0]
    q = SSD_TILE
    steps = seq // q
    w = SSD_WIDTH
    row = lambda off: pl.BlockSpec((q, w), lambda b, c: (b * steps + c, off))
    full = lambda r: pl.BlockSpec((r, w), lambda b, c: (0, 0))
    lane_row = pl.BlockSpec((1, LANES), lambda b, c: (0, 0))
    pad = lambda v: jnp.zeros((1, LANES), F32).at[0, DT_LANE0:DT_LANE0 + SSD_HEADS].set(v)
    blocks = 3 * _nbytes((q, w), F32) + _nbytes((q, LANES), F32) + _nbytes((q, w), BF16) \
        + 12 * _nbytes((1, w), F32)
    scratch_bytes = 2 * _nbytes((SUBLANES, w), F32) + _nbytes((SSD_GROUPS, SSD_STATE, w // SSD_GROUPS), F32)
    return pl.pallas_call(
        _ssd_kernel,
        out_shape=jax.ShapeDtypeStruct((t, w), BF16),
        grid=(bsz, steps),
        in_specs=[row(0), row(1), row(2),
                  pl.BlockSpec((q, LANES), lambda b, c: (b * steps + c, 0)),
                  full(SSD_CONV), full(1), full(SSD_CONV), full(1),
                  lane_row, lane_row, full(1), full(1)],
        out_specs=pl.BlockSpec((q, w), lambda b, c: (b * steps + c, 0)),
        scratch_shapes=[pltpu.VMEM((SUBLANES, w), F32), pltpu.VMEM((SUBLANES, w), F32),
                        pltpu.VMEM((SSD_GROUPS, SSD_STATE, w // SSD_GROUPS), F32)],
        compiler_params=_params(("arbitrary", "arbitrary"),
                                _vmem_limit(blocks, scratch_bytes + (24 << 20))),
        name="ssd_mixer",
    )(main, main, main, krdt,
      conv_w[:, :w], conv_b[:w].reshape(1, w), conv_w[:, w:], conv_b[w:].reshape(1, w),
      pad(dt_bias), pad(a_log),
      jnp.repeat(d_skip, SSD_HEAD_DIM).reshape(1, w), ssd_norm.reshape(1, w))


def _rglru_kernel(gy_ref, ux_ref, cw_ref, cb_ref, wa_ref, ba_ref, wi_ref, bi_ref, lam_ref,
                  o_ref, h_ref):
    seq, cols = ux_ref.shape
    tl = LRU_TILE
    nblk = cols // LRU_BLOCK
    w = cw_ref[...]
    cb = cb_ref[...]
    ba = ba_ref[...]
    bi = bi_ref[...]
    neg_c_sp = -LRU_C * _softplus(-lam_ref[...])
    sub = lax.broadcasted_iota(jnp.int32, (tl, cols), 0) % SUBLANES
    h_ref[...] = jnp.zeros_like(h_ref)

    def tile(ti, carry):
        t0 = pl.multiple_of(ti * tl, tl)
        blk = ux_ref[pl.ds(t0, tl), :]
        prev0 = pl.multiple_of(jnp.maximum(t0 - SUBLANES, 0), SUBLANES)
        hist = jnp.where(ti > 0, ux_ref[pl.ds(prev0, SUBLANES), :], 0.0)
        xp = jnp.concatenate([hist, blk], axis=0)
        u = cb
        for kk in range(LRU_CONV):
            off = SUBLANES - (LRU_CONV - 1) + kk
            u = u + w[kk:kk + 1] * xp[off:off + tl]
        ub = u.astype(BF16)
        ra, ia = [], []
        for j in range(nblk):
            cs = slice(j * LRU_BLOCK, (j + 1) * LRU_BLOCK)
            ra.append(_dot(ub[:, cs], wa_ref[j]))
            ia.append(_dot(ub[:, cs], wi_ref[j]))
        r_gate = jax.nn.sigmoid(jnp.concatenate(ra, axis=-1) + ba)
        i_gate = jax.nn.sigmoid(jnp.concatenate(ia, axis=-1) + bi)
        a = jnp.exp(neg_c_sp * r_gate)
        b = jnp.sqrt(1.0 - a * a) * (i_gate * u)

        for d in (1, 2, 4):
            ok = sub >= d
            a_sh = jnp.where(ok, pltpu.roll(a, d, 0), 1.0)
            b_sh = jnp.where(ok, pltpu.roll(b, d, 0), 0.0)
            b = b + a * b_sh
            a = a * a_sh

        h_in = h_ref[...]
        outs = []
        for r in range(tl // SUBLANES):
            rs = slice(r * SUBLANES, (r + 1) * SUBLANES)
            h_blk = b[rs] + a[rs] * h_in
            outs.append(h_blk)
            h_in = jnp.broadcast_to(h_blk[SUBLANES - 1:SUBLANES], (SUBLANES, cols))
        h_ref[...] = h_in
        hs = jnp.concatenate(outs, axis=0)
        gate = jax.nn.gelu(gy_ref[pl.ds(t0, tl), :])
        o_ref[pl.ds(t0, tl), :] = (hs * gate).astype(o_ref.dtype)
        return carry

    lax.fori_loop(0, seq // tl, tile, 0)


def _rglru(yx, conv_w, conv_b, w_a, b_a, w_i, b_i, lam, *, bsz, seq):
    t = yx.shape[0]
    r = conv_w.shape[1]
    cols = LRU_COLS
    ncb = r // cols
    nblk = cols // LRU_BLOCK
    vec = lambda: pl.BlockSpec((1, cols), lambda b, j: (0, j))
    wblk = pl.BlockSpec((nblk, LRU_BLOCK, LRU_BLOCK), lambda b, j: (j, 0, 0))
    blocks = 2 * _nbytes((seq, cols), F32) + _nbytes((seq, cols), BF16) \
        + 2 * _nbytes((nblk, LRU_BLOCK, LRU_BLOCK), BF16) + 8 * _nbytes((1, cols), F32)
    return pl.pallas_call(
        _rglru_kernel,
        out_shape=jax.ShapeDtypeStruct((t, r), BF16),
        grid=(bsz, ncb),
        in_specs=[pl.BlockSpec((seq, cols), lambda b, j: (b, j)),
                  pl.BlockSpec((seq, cols), lambda b, j: (b, ncb + j)),
                  pl.BlockSpec((LRU_CONV, cols), lambda b, j: (0, j)), vec(),
                  wblk, vec(), wblk, vec(), vec()],
        out_specs=pl.BlockSpec((seq, cols), lambda b, j: (b, j)),
        scratch_shapes=[pltpu.VMEM((SUBLANES, cols), F32)],
        compiler_params=_params(("arbitrary", "arbitrary"), _vmem_limit(blocks, 16 << 20)),
        name="rglru_mixer",
    )(yx, yx, conv_w, conv_b.reshape(1, r), w_a.astype(BF16), b_a.reshape(1, r),
      w_i.astype(BF16), b_i.reshape(1, r), lam.reshape(1, r))


def _even_in_weights(w_in):
    d = w_in.shape[0]
    o_z, o_xbc = 0, SSD_WIDTH
    o_dt = o_xbc + SSD_WIDTH + 2 * SSD_BC_WIDTH
    o_cq = o_dt + SSD_HEADS
    q_rank = d // 4
    kv_rank = d // 8
    o_ckv = o_cq + q_rank
    o_kr = o_ckv + kv_rank
    w_main = jnp.concatenate([w_in[:, o_z:o_dt], w_in[:, o_cq:o_kr]], axis=1).astype(BF16)
    w_small = jnp.concatenate(
        [w_in[:, o_kr:o_kr + MLA_ROPE], w_in[:, o_dt:o_cq],
         jnp.zeros((d, LANES - MLA_ROPE - SSD_HEADS), w_in.dtype)], axis=1).astype(BF16)
    return w_main, w_small, q_rank, kv_rank


def _q_weights(w_uq):
    r = w_uq.shape[0]
    w = w_uq.reshape(r, MLA_HEADS, MLA_NOPE + MLA_ROPE)
    nope = w[:, :, :MLA_NOPE].reshape(r, MLA_HEADS * MLA_NOPE)
    rope = jnp.pad(w[:, :, MLA_NOPE:], ((0, 0), (0, 0), (0, LANES - MLA_ROPE)))
    return jnp.concatenate([nope, rope.reshape(r, MLA_HEADS * LANES)], axis=1).astype(BF16)


def _kv_weights(w_ukv):
    r = w_ukv.shape[0]
    w = w_ukv.reshape(r, MLA_HEADS, MLA_NOPE + MLA_V_DIM)
    k = w[:, :, :MLA_NOPE].reshape(r, MLA_HEADS * MLA_NOPE)
    v = w[:, :, MLA_NOPE:].reshape(r, MLA_HEADS * MLA_V_DIM)
    return jnp.concatenate([k, v], axis=1).astype(BF16)


def kernel(x, c, positions, ada_w, ada_b, ada_table, norm_mix, norm_ffn, norm_final, ffn_up, ffn_down, e_w_in, e_conv_w, e_conv_b, e_dt_bias, e_a_log, e_d_skip, e_ssd_norm, e_q_norm, e_w_uq, e_kv_norm, e_w_ukv, e_w_out, o_w_y, o_b_y, o_w_x, o_b_x, o_conv_w, o_conv_b, o_w_a, o_b_a, o_w_i, o_b_i, o_lam, o_w_out, o_b_out):
    bsz, seq, d = x.shape
    t = bsz * seq
    depth = ada_table.shape[0]
    xf = x.reshape(t, d)

    cc, ss = _rope_tables(positions)
    mod_all = _ada_proj(c, ada_w, ada_b).reshape(bsz, N_MOD, d)

    for layer in range(depth):
        mod = mod_all + ada_table[layer]
        shift_m, scale_m, gate_m, shift_f, scale_f, gate_f = (mod[:, i:i + 1] for i in range(N_MOD))
        hm = _rownorm(xf, norm_mix[layer], scale=scale_m, shift=shift_m, rows_per_batch=seq)
        j = layer // 2
        if layer % 2 == 0:
            w_main, w_small, q_rank, kv_rank = _even_in_weights(e_w_in[j])
            main = _mm(hm, w_main, tn=768)
            krdt = _mm(hm, w_small, tn=LANES)
            o_cq = 2 * SSD_WIDTH + 2 * SSD_BC_WIDTH
            cqn = _rownorm(main, e_q_norm[j], col_block=o_cq // q_rank, width=q_rank)
            ckvn = _rownorm(main, e_kv_norm[j], col_block=(o_cq + q_rank) // kv_rank, width=kv_rank)
            q = _mm(cqn, _q_weights(e_w_uq[j]))
            kv = _mm(ckvn, _kv_weights(e_w_ukv[j]))
            o_attn = _attention(q, kv, krdt, cc, ss, bsz=bsz, seq=seq)
            y_ssd = _ssd(main, krdt, e_conv_w[j], e_conv_b[j], e_dt_bias[j], e_a_log[j],
                         e_d_skip[j], e_ssd_norm[j], bsz=bsz, seq=seq)
            w_out = e_w_out[j].astype(BF16)
            xf = _mm_res([(y_ssd, w_out[:SSD_WIDTH]), (o_attn, w_out[SSD_WIDTH:])], xf, gate_m,
                         rows_per_batch=seq)
        else:
            w_yx = jnp.concatenate([o_w_y[j], o_w_x[j]], axis=1).astype(BF16)
            b_yx = jnp.concatenate([o_b_y[j], o_b_x[j]])
            yx = _mm(hm, w_yx, bias=b_yx)
            y_lru = _rglru(yx, o_conv_w[j], o_conv_b[j], o_w_a[j], o_b_a[j], o_w_i[j], o_b_i[j],
                           o_lam[j], bsz=bsz, seq=seq)
            r = y_lru.shape[1]
            xf = _mm_res([(y_lru, o_w_out[j].astype(BF16))], xf, gate_m, bias=o_b_out[j],
                         rows_per_batch=seq, tk=r // 2)
        hf = _rownorm(xf, norm_ffn[layer], scale=scale_f, shift=shift_f, rows_per_batch=seq)
        hid = _mm(hf, ffn_up[layer].astype(BF16), act="relu2", out_dtype=BF16)
        xf = _mm_res([(hid, ffn_down[layer].astype(BF16))], xf, gate_f, rows_per_batch=seq,
                     tk=2048)

    out = _rownorm(xf, norm_final, out_dtype=F32)
    return out.reshape(bsz, seq, d)
```

```python
import functools

import jax
import jax.numpy as jnp
from jax import lax
from jax.experimental import pallas as pl
from jax.experimental.pallas import tpu as pltpu

F32 = jnp.float32
BF16 = jnp.bfloat16

EPS = 1e-6
ROPE_THETA = 10000.0
N_MOD = 6

SSD_HEAD_DIM = 64
SSD_GROUPS = 8
SSD_HEADS_PER_GROUP = 4
SSD_STATE = 128
SSD_CONV = 4
SSD_HEADS = SSD_GROUPS * SSD_HEADS_PER_GROUP
SSD_WIDTH = SSD_HEADS * SSD_HEAD_DIM
SSD_BC_WIDTH = SSD_GROUPS * SSD_STATE
SSD_TILE = 256

MLA_HEADS = 16
MLA_NOPE = 128
MLA_ROPE = 64
MLA_V_DIM = 128
MLA_STREAM_CHUNK = 64
MLA_Q_TILE = 256

LRU_BLOCK = 256
LRU_CONV = 4
LRU_C = 8.0
LRU_TILE = 256
LRU_COLS = 512

LANES = 128
SUBLANES = 8
DT_LANE0 = 64
VMEM_LIMIT_CAP = 60000 * 1024


def _vmem_limit(block_bytes, extra=0):
    need = 2 * block_bytes + extra + (4 << 20)
    return int(min(max(need, 16 << 20), VMEM_LIMIT_CAP))


def _params(semantics, vmem):
    return pltpu.CompilerParams(dimension_semantics=semantics, vmem_limit_bytes=vmem)


def _nbytes(shape, dtype):
    n = 1
    for s in shape:
        n *= s
    return n * jnp.dtype(dtype).itemsize


def _silu(x):
    return x * jax.nn.sigmoid(x)


def _softplus(x):
    return jnp.maximum(x, 0.0) + jnp.log(1.0 + jnp.exp(-jnp.abs(x)))


def _split3(v):
    a1 = v.astype(BF16)
    r1 = v - a1.astype(F32)
    a2 = r1.astype(BF16)
    a3 = (r1 - a2.astype(F32)).astype(BF16)
    return a1, a2, a3


def _causal_conv(hist, blk, w, bias):
    rows, cols = blk.shape
    taps = w.shape[0]
    groups = rows // SUBLANES
    x3 = jnp.concatenate([hist, blk], axis=0).reshape(groups + 1, SUBLANES, cols)
    sub = lax.broadcasted_iota(jnp.int32, (groups, SUBLANES, cols), 1)
    y = bias + w[taps - 1:taps] * x3[1:]
    for s in range(1, taps):
        r = pltpu.roll(x3, s, 1)
        y = y + w[taps - 1 - s:taps - s] * jnp.where(sub >= s, r[1:], r[:-1])
    return y.reshape(rows, cols)


def _dot(a, b):
    return jnp.dot(a, b, preferred_element_type=F32)


def _dot_nt(a, b):
    return lax.dot_general(a, b, (((1,), (1,)), ((), ())), preferred_element_type=F32)


def _dot_tn(a, b):
    return lax.dot_general(a, b, (((0,), (0,)), ((), ())), preferred_element_type=F32)


def _select_dot(sel, v):
    a1, a2, a3 = _split3(v)
    return _dot(sel, a1) + _dot(sel, a2) + _dot(sel, a3)


def _dot_select(v, sel):
    a1, a2, a3 = _split3(v)
    return _dot(a1, sel) + _dot(a2, sel) + _dot(a3, sel)


def _ada_kernel(c_ref, w_ref, b_ref, o_ref):
    s = _silu(c_ref[...]).astype(BF16)
    o_ref[...] = _dot(s, w_ref[...].astype(BF16)) + b_ref[...]


def _ada_proj(c, w, b):
    bsz, d = c.shape
    n = w.shape[1]
    tn = 512
    blocks = _nbytes((d, tn), F32) + _nbytes((bsz, d), F32) + 2 * _nbytes((bsz, tn), F32)
    return pl.pallas_call(
        _ada_kernel,
        out_shape=jax.ShapeDtypeStruct((bsz, n), F32),
        grid=(n // tn,),
        in_specs=[pl.BlockSpec((bsz, d), lambda j: (0, 0)),
                  pl.BlockSpec((d, tn), lambda j: (0, j)),
                  pl.BlockSpec((1, tn), lambda j: (0, j))],
        out_specs=pl.BlockSpec((bsz, tn), lambda j: (0, j)),
        compiler_params=_params(("arbitrary",), _vmem_limit(blocks, _nbytes((d, tn), BF16))),
        name="ada_proj",
    )(c, w, b.reshape(1, n))


def _rope_table_kernel(pos_ref, freq_ref, cc_ref, ss_ref):
    ang = pos_ref[...].astype(F32) * freq_ref[...]
    lane = lax.broadcasted_iota(jnp.int32, ang.shape, 1)
    half = MLA_ROPE // 2
    c = jnp.cos(ang)
    s = jnp.sin(ang)
    cc_ref[...] = jnp.where(lane < MLA_ROPE, c, 0.0)
    ss_ref[...] = jnp.where(lane < half, -s, jnp.where(lane < MLA_ROPE, s, 0.0))


def _rope_tables(positions):
    t = positions.size
    half = MLA_ROPE // 2
    inv_freq = ROPE_THETA ** (-jnp.arange(0, MLA_ROPE, 2, dtype=F32) / MLA_ROPE)
    freq = jnp.concatenate([inv_freq, inv_freq, jnp.zeros((LANES - 2 * half,), F32)]).reshape(1, LANES)
    tm = min(t, 2048)
    return pl.pallas_call(
        _rope_table_kernel,
        out_shape=(jax.ShapeDtypeStruct((t, LANES), F32), jax.ShapeDtypeStruct((t, LANES), F32)),
        grid=(t // tm,),
        in_specs=[pl.BlockSpec((tm, 1), lambda i: (i, 0)),
                  pl.BlockSpec((1, LANES), lambda i: (0, 0))],
        out_specs=(pl.BlockSpec((tm, LANES), lambda i: (i, 0)),
                   pl.BlockSpec((tm, LANES), lambda i: (i, 0))),
        compiler_params=_params(("arbitrary",), _vmem_limit(4 * _nbytes((tm, LANES), F32))),
        name="rope_tables",
    )(positions.reshape(t, 1), freq)


def _rownorm_kernel(x_ref, g_ref, *rest, modulate):
    o_ref = rest[-1]
    x = x_ref[...]
    y = x * lax.rsqrt(jnp.mean(x * x, axis=-1, keepdims=True) + EPS) * g_ref[...]
    if modulate:
        sc_ref, sh_ref = rest[0], rest[1]
        y = y * (1.0 + sc_ref[...]) + sh_ref[...]
    o_ref[...] = y.astype(o_ref.dtype)


def _rownorm(x, gain, *, col_block=0, width=None, scale=None, shift=None, rows_per_batch=None,
             out_dtype=BF16):
    t = x.shape[0]
    width = x.shape[1] if width is None else width
    tm = min(512, t)
    modulate = scale is not None
    in_specs = [pl.BlockSpec((tm, width), lambda i: (i, col_block)),
                pl.BlockSpec((1, width), lambda i: (0, 0))]
    args = [x, gain.reshape(1, width)]
    if modulate:
        bmap = lambda i: ((i * tm) // rows_per_batch, 0, 0)
        in_specs += [pl.BlockSpec((None, 1, width), bmap), pl.BlockSpec((None, 1, width), bmap)]
        args += [scale, shift]
    blocks = _nbytes((tm, width), F32) + _nbytes((tm, width), out_dtype) + 3 * _nbytes((1, width), F32)
    return pl.pallas_call(
        functools.partial(_rownorm_kernel, modulate=modulate),
        out_shape=jax.ShapeDtypeStruct((t, width), out_dtype),
        grid=(t // tm,),
        in_specs=in_specs,
        out_specs=pl.BlockSpec((tm, width), lambda i: (i, 0)),
        compiler_params=_params(("arbitrary",), _vmem_limit(blocks, 2 * _nbytes((tm, width), F32))),
        name="rownorm",
    )(*args)


def _mm_kernel(x_ref, w_ref, *rest, act, has_bias, n_cast):
    pos = int(has_bias)
    cast_in = rest[pos:pos + n_cast]
    o_ref = rest[pos + n_cast]
    cast_out = rest[pos + n_cast + 1:]
    acc = _dot(x_ref[...], w_ref[...])
    if has_bias:
        acc = acc + rest[0][...]
    if act == "relu2":
        r = jnp.maximum(acc, 0.0)
        acc = r * r
    o_ref[...] = acc.astype(o_ref.dtype)
    for src, dst in zip(cast_in, cast_out):
        dst[...] = src[...].astype(dst.dtype)


def _weight_spec(w, layer, rows, cols, index):
    if w.ndim == 2:
        return pl.BlockSpec((rows, cols), index)
    return pl.BlockSpec((None, rows, cols), lambda *g: (layer,) + tuple(index(*g)))


def _mm(x, w, *, layer=None, bias=None, act=None, out_dtype=F32, tm=1024, tn=1024, cast=()):
    m, k = x.shape
    n = w.shape[-1]
    tm = min(tm, m)
    tn = min(tn, n)
    gj = n // tn
    steps = (m // tm) * gj
    has_bias = bias is not None
    in_specs = [pl.BlockSpec((tm, k), lambda i, j: (i, 0)),
                _weight_spec(w, layer, k, tn, lambda i, j: (0, j))]
    args = [x, w]
    if has_bias:
        in_specs.append(pl.BlockSpec((1, tn), lambda i, j: (0, j)))
        args.append(bias.reshape(1, n))
    blocks = _nbytes((tm, k), x.dtype) + _nbytes((k, tn), w.dtype) + _nbytes((tm, tn), out_dtype)
    out_shape = [jax.ShapeDtypeStruct((m, n), out_dtype)]
    out_specs = [pl.BlockSpec((tm, tn), lambda i, j: (i, j))]
    for src, src_layer in cast:
        _, rows, cols = src.shape
        slab = rows // steps
        in_specs.append(pl.BlockSpec((None, slab, cols),
                                     lambda i, j, src_layer=src_layer: (src_layer, i * gj + j, 0)))
        args.append(src)
        out_shape.append(jax.ShapeDtypeStruct((rows, cols), BF16))
        out_specs.append(pl.BlockSpec((slab, cols), lambda i, j: (i * gj + j, 0)))
        blocks += _nbytes((slab, cols), F32) + _nbytes((slab, cols), BF16)
    outs = pl.pallas_call(
        functools.partial(_mm_kernel, act=act, has_bias=has_bias, n_cast=len(cast)),
        out_shape=out_shape,
        grid=(m // tm, gj),
        in_specs=in_specs,
        out_specs=out_specs,
        compiler_params=_params(("arbitrary", "arbitrary"),
                                _vmem_limit(blocks, 3 * _nbytes((tm, tn), F32))),
        name="mm",
    )(*args)
    return outs if cast else outs[0]


def _mm_res_kernel(*refs, n_pairs, has_bias, nk):
    xs = refs[0:2 * n_pairs:2]
    ws = refs[1:2 * n_pairs:2]
    pos = 2 * n_pairs
    res_ref, gate_ref = refs[pos], refs[pos + 1]
    pos += 2
    bias_ref = refs[pos] if has_bias else None
    pos += int(has_bias)
    o_ref = refs[pos]
    acc_ref = refs[pos + 1] if nk > 1 else None

    def partial_sum():
        acc = _dot(xs[0][...], ws[0][...])
        for x_ref, w_ref in zip(xs[1:], ws[1:]):
            acc = acc + _dot(x_ref[...], w_ref[...])
        return acc

    def finish(acc):
        if has_bias:
            acc = acc + bias_ref[...]
        o_ref[...] = res_ref[...] + gate_ref[...] * acc

    if nk == 1:
        finish(partial_sum())
        return

    kk = pl.program_id(2)

    @pl.when(kk == 0)
    def _():
        acc_ref[...] = partial_sum()

    @pl.when(kk > 0)
    def _():
        acc_ref[...] += partial_sum()

    @pl.when(kk == nk - 1)
    def _():
        finish(acc_ref[...])


def _mm_res(pairs, res, gate, *, layer=None, bias=None, rows_per_batch, tm=1024, tn=1024, tk=None):
    m, n = res.shape
    k = pairs[0][0].shape[1]
    tk = k if tk is None else tk
    nk = k // tk
    tm = min(tm, rows_per_batch)
    has_bias = bias is not None
    in_specs, args = [], []
    blocks = 0
    for x, w, row0 in pairs:
        in_specs += [pl.BlockSpec((tm, tk), lambda i, j, kk: (i, kk)),
                     _weight_spec(w, layer, tk, tn, lambda i, j, kk, row0=row0: (row0 + kk, j))]
        args += [x, w]
        blocks += _nbytes((tm, tk), x.dtype) + _nbytes((tk, tn), w.dtype)
    in_specs += [pl.BlockSpec((tm, tn), lambda i, j, kk: (i, j)),
                 pl.BlockSpec((None, 1, tn), lambda i, j, kk: ((i * tm) // rows_per_batch, 0, j))]
    args += [res, gate]
    blocks += 2 * _nbytes((tm, tn), F32)
    if has_bias:
        in_specs.append(pl.BlockSpec((1, tn), lambda i, j, kk: (0, j)))
        args.append(bias.reshape(1, n))
    scratch = [pltpu.VMEM((tm, tn), F32)] if nk > 1 else []
    return pl.pallas_call(
        functools.partial(_mm_res_kernel, n_pairs=len(pairs), has_bias=has_bias, nk=nk),
        out_shape=jax.ShapeDtypeStruct((m, n), F32),
        grid=(m // tm, n // tn, nk),
        in_specs=in_specs,
        out_specs=pl.BlockSpec((tm, tn), lambda i, j, kk: (i, j)),
        scratch_shapes=scratch,
        compiler_params=_params(("arbitrary", "arbitrary", "arbitrary"),
                                _vmem_limit(blocks, 4 * _nbytes((tm, tn), F32))),
        name="mm_res",
    )(*args)


def _attn_kernel(qn_ref, qr_ref, kn_ref, v_ref, kr_ref, cc_ref, ss_ref, o_ref, *, scale, tq):
    seq = qn_ref.shape[0]
    cc = cc_ref[...]
    ss = ss_ref[...]
    lane = lax.broadcasted_iota(jnp.int32, (seq, LANES), 1)
    half = MLA_ROPE // 2

    def rope(x):
        swapped = jnp.where(lane < half, pltpu.roll(x, LANES - half, 1), pltpu.roll(x, half, 1))
        return x * cc + swapped * ss

    qf = jnp.concatenate([(qn_ref[...] * scale).astype(BF16),
                          (rope(qr_ref[...]) * scale).astype(BF16)], axis=-1)
    kf = jnp.concatenate([kn_ref[...].astype(BF16), rope(kr_ref[...]).astype(BF16)], axis=-1)
    v = v_ref[...].astype(BF16)

    r_chunk = lax.broadcasted_iota(jnp.int32, (tq, tq), 0) // MLA_STREAM_CHUNK
    c_chunk = lax.broadcasted_iota(jnp.int32, (tq, tq), 1) // MLA_STREAM_CHUNK
    diag_ok = c_chunk <= r_chunk

    for i in range(seq // tq):
        q0, q1 = i * tq, (i + 1) * tq
        s_d = jnp.where(diag_ok, _dot_nt(qf[q0:q1], kf[q0:q1]), -jnp.inf)
        m = jnp.max(s_d, axis=-1, keepdims=True)
        if i > 0:
            s_o = _dot_nt(qf[q0:q1], kf[:q0])
            m = jnp.maximum(m, jnp.max(s_o, axis=-1, keepdims=True))
        p_d = jnp.exp(s_d - m)
        denom = jnp.sum(p_d, axis=-1, keepdims=True)
        acc = _dot(p_d.astype(BF16), v[q0:q1])
        if i > 0:
            p_o = jnp.exp(s_o - m)
            denom = denom + jnp.sum(p_o, axis=-1, keepdims=True)
            acc = acc + _dot(p_o.astype(BF16), v[:q0])
        o_ref[q0:q1, :] = (acc / denom).astype(o_ref.dtype)


def _attention(q, kv, krdt, cc, ss, *, bsz, seq):
    t = q.shape[0]
    scale = (MLA_NOPE + MLA_ROPE) ** -0.5
    blk = lambda off: pl.BlockSpec((seq, LANES), lambda b, h: (b, off + h))
    tok = pl.BlockSpec((seq, LANES), lambda b, h: (b, 0))
    blocks = 7 * _nbytes((seq, LANES), F32) + _nbytes((seq, LANES), BF16)
    return pl.pallas_call(
        functools.partial(_attn_kernel, scale=scale, tq=MLA_Q_TILE),
        out_shape=jax.ShapeDtypeStruct((t, MLA_HEADS * MLA_V_DIM), BF16),
        grid=(bsz, MLA_HEADS),
        in_specs=[blk(0), blk(MLA_HEADS), blk(0), blk(MLA_HEADS), tok, tok, tok],
        out_specs=pl.BlockSpec((seq, MLA_V_DIM), lambda b, h: (b, h)),
        compiler_params=_params(("arbitrary", "arbitrary"), _vmem_limit(blocks, 16 << 20)),
        name="mla_attention",
    )(q, q, kv, kv, krdt, cc, ss)


def _ssd_kernel(z_ref, x_ref, bc_ref, dk_ref, cwx_ref, cbx_ref, cwbc_ref, cbbc_ref,
                dtb_ref, alog_ref, dskip_ref, gain_ref, o_ref,
                tailx_ref, tailbc_ref, state_ref):
    q = x_ref.shape[0]
    p = SSD_HEAD_DIM
    gw = SSD_HEADS_PER_GROUP * p
    n = SSD_STATE

    @pl.when(pl.program_id(1) == 0)
    def _():
        tailx_ref[...] = jnp.zeros_like(tailx_ref)
        tailbc_ref[...] = jnp.zeros_like(tailbc_ref)
        state_ref[...] = jnp.zeros_like(state_ref)

    def conv_silu(tail_ref, blk_ref, w_ref, b_ref):
        blk = blk_ref[...]
        y = _causal_conv(tail_ref[...], blk, w_ref[...], b_ref[...])
        tail_ref[...] = blk[q - SUBLANES:]
        return _silu(y)

    xc = conv_silu(tailx_ref, x_ref, cwx_ref, cbx_ref)
    bcc = conv_silu(tailbc_ref, bc_ref, cwbc_ref, cbbc_ref)

    lane = lax.broadcasted_iota(jnp.int32, (q, LANES), 1)
    lane1 = lax.broadcasted_iota(jnp.int32, (1, LANES), 1)
    is_dt = (lane1 >= DT_LANE0) & (lane1 < DT_LANE0 + SSD_HEADS)
    dt = _softplus(dk_ref[...] + dtb_ref[...])
    a_row = jnp.where(is_dt, -jnp.exp(alog_ref[...]), 0.0)
    ad = dt * a_row

    rr = lax.broadcasted_iota(jnp.int32, (q, q), 0)
    cc_ = lax.broadcasted_iota(jnp.int32, (q, q), 1)
    causal = cc_ <= rr
    tri = jnp.where(causal, 1.0, 0.0).astype(BF16)
    a_cs = _select_dot(tri, ad)
    a_cs_t = a_cs.T

    er = lax.broadcasted_iota(jnp.int32, (LANES, SSD_WIDTH), 0)
    ec = lax.broadcasted_iota(jnp.int32, (LANES, SSD_WIDTH), 1)
    expand = jnp.where(er - DT_LANE0 == ec // p, 1.0, 0.0).astype(BF16)
    dt_x = _dot_select(dt, expand)
    acs_x = _dot_select(a_cs, expand)
    dec_x = jnp.exp(acs_x)
    aend_x = acs_x[q - 1:q, :]
    xs = xc * dt_x
    xs_end = xs * jnp.exp(aend_x - acs_x)
    chunk_decay = dec_x[q - 1:q, :]

    lo_half = lane < p
    zs = _silu(z_ref[...])
    dskip = dskip_ref[...]
    gain = gain_ref[...]

    for g in range(SSD_GROUPS):
        gs = slice(g * gw, (g + 1) * gw)
        b_g = bcc[:, g * n:(g + 1) * n].astype(BF16)
        c_g = bcc[:, SSD_BC_WIDTH + g * n:SSD_BC_WIDTH + (g + 1) * n].astype(BF16)
        cb = _dot_nt(c_g, b_g)
        s_prev = state_ref[g]
        y_off = _dot(c_g, s_prev.astype(BF16)) * dec_x[:, gs]
        s_new = _dot_tn(b_g, xs_end[:, gs].astype(BF16))
        state_ref[g] = chunk_decay[:, gs] * s_prev + s_new

        pair_out = []
        for pr in range(SSD_HEADS_PER_GROUP // 2):
            ps = slice(g * gw + pr * LANES, g * gw + (pr + 1) * LANES)
            xs_pair = xs[:, ps]
            acc = None
            for hh in range(2):
                h = g * SSD_HEADS_PER_GROUP + pr * 2 + hh
                col = jnp.sum(jnp.where(lane == DT_LANE0 + h, a_cs, 0.0), axis=-1, keepdims=True)
                row = a_cs_t[DT_LANE0 + h:DT_LANE0 + h + 1, :]
                seg = jnp.exp(jnp.where(causal, col - row, -jnp.inf))
                mh = (cb * seg).astype(BF16)
                keep = lo_half if hh == 0 else jnp.logical_not(lo_half)
                term = _dot(mh, jnp.where(keep, xs_pair, 0.0).astype(BF16))
                acc = term if acc is None else acc + term
            pair_out.append(acc)
        y = jnp.concatenate(pair_out, axis=-1) + y_off + dskip[:, gs] * xc[:, gs]
        y = y * zs[:, gs]
        y = y * lax.rsqrt(jnp.mean(y * y, axis=-1, keepdims=True) + EPS) * gain[:, gs]
        o_ref[:, gs] = y.astype(o_ref.dtype)


def _ssd(main, krdt, conv_w, conv_b, dt_bias, a_log, d_skip, ssd_norm, *, bsz, seq):
    t = main.shape[0]
    q = SSD_TILE
    steps = seq // q
    w = SSD_WIDTH
    row = lambda off: pl.BlockSpec((q, w), lambda b, c: (b * steps + c, off))
    full = lambda r: pl.BlockSpec((r, w), lambda b, c: (0, 0))
    lane_row = pl.BlockSpec((1, LANES), lambda b, c: (0, 0))
    pad = lambda v: jnp.zeros((1, LANES), F32).at[0, DT_LANE0:DT_LANE0 + SSD_HEADS].set(v)
    blocks = 3 * _nbytes((q, w), F32) + _nbytes((q, LANES), F32) + _nbytes((q, w), BF16) \
        + 12 * _nbytes((1, w), F32)
    scratch_bytes = 2 * _nbytes((SUBLANES, w), F32) + _nbytes((SSD_GROUPS, SSD_STATE, w // SSD_GROUPS), F32)
    return pl.pallas_call(
        _ssd_kernel,
        out_shape=jax.ShapeDtypeStruct((t, w), BF16),
        grid=(bsz, steps),
        in_specs=[row(0), row(1), row(2),
                  pl.BlockSpec((q, LANES), lambda b, c: (b * steps + c, 0)),
                  full(SSD_CONV), full(1), full(SSD_CONV), full(1),
                  lane_row, lane_row, full(1), full(1)],
        out_specs=pl.BlockSpec((q, w), lambda b, c: (b * steps + c, 0)),
        scratch_shapes=[pltpu.VMEM((SUBLANES, w), F32), pltpu.VMEM((SUBLANES, w), F32),
                        pltpu.VMEM((SSD_GROUPS, SSD_STATE, w // SSD_GROUPS), F32)],
        compiler_params=_params(("arbitrary", "arbitrary"),
                                _vmem_limit(blocks, scratch_bytes + (24 << 20))),
        name="ssd_mixer",
    )(main, main, main, krdt,
      conv_w[:, :w], conv_b[:w].reshape(1, w), conv_w[:, w:], conv_b[w:].reshape(1, w),
      pad(dt_bias), pad(a_log),
      jnp.repeat(d_skip, SSD_HEAD_DIM).reshape(1, w), ssd_norm.reshape(1, w))


def _rglru_kernel(gy_ref, ux_ref, cw_ref, cb_ref, wa_ref, ba_ref, wi_ref, bi_ref, lam_ref,
                  o_ref, h_ref):
    seq, cols = ux_ref.shape
    tl = LRU_TILE
    groups = tl // SUBLANES
    nblk = cols // LRU_BLOCK
    w = cw_ref[...]
    cb = cb_ref[...]
    ba = ba_ref[...]
    bi = bi_ref[...]
    neg_c_sp = -LRU_C * _softplus(-lam_ref[...])
    sub = lax.broadcasted_iota(jnp.int32, (groups, SUBLANES, cols), 1)
    h_ref[...] = jnp.zeros_like(h_ref)

    def tile(ti, carry):
        t0 = pl.multiple_of(ti * tl, tl)
        blk = ux_ref[pl.ds(t0, tl), :]
        prev0 = pl.multiple_of(jnp.maximum(t0 - SUBLANES, 0), SUBLANES)
        hist = jnp.where(ti > 0, ux_ref[pl.ds(prev0, SUBLANES), :], 0.0)
        u = _causal_conv(hist, blk, w, cb)
        ub = u.astype(BF16)
        ra, ia = [], []
        for j in range(nblk):
            cs = slice(j * LRU_BLOCK, (j + 1) * LRU_BLOCK)
            ra.append(_dot(ub[:, cs], wa_ref[j]))
            ia.append(_dot(ub[:, cs], wi_ref[j]))
        r_gate = jax.nn.sigmoid(jnp.concatenate(ra, axis=-1) + ba)
        i_gate = jax.nn.sigmoid(jnp.concatenate(ia, axis=-1) + bi)
        a2 = jnp.exp(neg_c_sp * r_gate)
        b2 = jnp.sqrt(1.0 - a2 * a2) * (i_gate * u)

        a = a2.reshape(groups, SUBLANES, cols)
        b = b2.reshape(groups, SUBLANES, cols)
        for d in (1, 2, 4):
            ok = sub >= d
            a_sh = jnp.where(ok, pltpu.roll(a, d, 1), 1.0)
            b_sh = jnp.where(ok, pltpu.roll(b, d, 1), 0.0)
            b = b + a * b_sh
            a = a * a_sh

        h_in = h_ref[...]
        outs = []
        for r in range(groups):
            h_blk = b[r] + a[r] * h_in
            outs.append(h_blk)
            h_in = jnp.broadcast_to(h_blk[SUBLANES - 1:SUBLANES], (SUBLANES, cols))
        h_ref[...] = h_in
        hs = jnp.concatenate(outs, axis=0)
        gate = jax.nn.gelu(gy_ref[pl.ds(t0, tl), :])
        o_ref[pl.ds(t0, tl), :] = (hs * gate).astype(o_ref.dtype)
        return carry

    lax.fori_loop(0, seq // tl, tile, 0)


def _rglru(gy, ux, layer, conv_w, conv_b, w_a, b_a, w_i, b_i, lam, *, bsz, seq):
    t, r = ux.shape
    cols = LRU_COLS
    nblk = cols // LRU_BLOCK
    vec = lambda: pl.BlockSpec((1, cols), lambda b, j: (0, j))
    wblk = pl.BlockSpec((None, nblk, LRU_BLOCK, LRU_BLOCK), lambda b, j: (layer, j, 0, 0))
    blocks = 2 * _nbytes((seq, cols), F32) + _nbytes((seq, cols), BF16) \
        + 2 * _nbytes((nblk, LRU_BLOCK, LRU_BLOCK), BF16) + 8 * _nbytes((1, cols), F32)
    return pl.pallas_call(
        _rglru_kernel,
        out_shape=jax.ShapeDtypeStruct((t, r), BF16),
        grid=(bsz, r // cols),
        in_specs=[pl.BlockSpec((seq, cols), lambda b, j: (b, j)),
                  pl.BlockSpec((seq, cols), lambda b, j: (b, j)),
                  pl.BlockSpec((LRU_CONV, cols), lambda b, j: (0, j)), vec(),
                  wblk, vec(), wblk, vec(), vec()],
        out_specs=pl.BlockSpec((seq, cols), lambda b, j: (b, j)),
        scratch_shapes=[pltpu.VMEM((SUBLANES, cols), F32)],
        compiler_params=_params(("arbitrary", "arbitrary"), _vmem_limit(blocks, 16 << 20)),
        name="rglru_mixer",
    )(gy, ux, conv_w, conv_b.reshape(1, r), w_a, b_a.reshape(1, r), w_i, b_i.reshape(1, r),
      lam.reshape(1, r))


def _even_in_weights(w_in):
    d = w_in.shape[0]
    o_z, o_xbc = 0, SSD_WIDTH
    o_dt = o_xbc + SSD_WIDTH + 2 * SSD_BC_WIDTH
    o_cq = o_dt + SSD_HEADS
    q_rank = d // 4
    kv_rank = d // 8
    o_ckv = o_cq + q_rank
    o_kr = o_ckv + kv_rank
    w_main = jnp.concatenate([w_in[:, o_z:o_dt], w_in[:, o_cq:o_kr]], axis=1).astype(BF16)
    w_small = jnp.concatenate(
        [w_in[:, o_kr:o_kr + MLA_ROPE], w_in[:, o_dt:o_cq],
         jnp.zeros((d, LANES - MLA_ROPE - SSD_HEADS), w_in.dtype)], axis=1).astype(BF16)
    return w_main, w_small, q_rank, kv_rank


def _q_weights(w_uq):
    r = w_uq.shape[0]
    w = w_uq.reshape(r, MLA_HEADS, MLA_NOPE + MLA_ROPE)
    nope = w[:, :, :MLA_NOPE].reshape(r, MLA_HEADS * MLA_NOPE)
    rope = jnp.pad(w[:, :, MLA_NOPE:], ((0, 0), (0, 0), (0, LANES - MLA_ROPE)))
    return jnp.concatenate([nope, rope.reshape(r, MLA_HEADS * LANES)], axis=1).astype(BF16)


def _kv_weights(w_ukv):
    r = w_ukv.shape[0]
    w = w_ukv.reshape(r, MLA_HEADS, MLA_NOPE + MLA_V_DIM)
    k = w[:, :, :MLA_NOPE].reshape(r, MLA_HEADS * MLA_NOPE)
    v = w[:, :, MLA_NOPE:].reshape(r, MLA_HEADS * MLA_V_DIM)
    return jnp.concatenate([k, v], axis=1).astype(BF16)


def kernel(x, c, positions, ada_w, ada_b, ada_table, norm_mix, norm_ffn, norm_final, ffn_up, ffn_down, e_w_in, e_conv_w, e_conv_b, e_dt_bias, e_a_log, e_d_skip, e_ssd_norm, e_q_norm, e_w_uq, e_kv_norm, e_w_ukv, e_w_out, o_w_y, o_b_y, o_w_x, o_b_x, o_conv_w, o_conv_b, o_w_a, o_b_a, o_w_i, o_b_i, o_lam, o_w_out, o_b_out):
    bsz, seq, d = x.shape
    t = bsz * seq
    depth = ada_table.shape[0]
    xf = x.reshape(t, d)

    cc, ss = _rope_tables(positions)
    mod_all = _ada_proj(c, ada_w, ada_b).reshape(bsz, N_MOD, d)

    e_w_out_b = e_w_out.astype(BF16)
    o_w_y_b, o_w_x_b, o_w_out_b = o_w_y.astype(BF16), o_w_x.astype(BF16), o_w_out.astype(BF16)
    o_w_a_b, o_w_i_b = o_w_a.astype(BF16), o_w_i.astype(BF16)
    up_b, down_b = ffn_up[0].astype(BF16), ffn_down[0].astype(BF16)

    for layer in range(depth):
        mod = mod_all + ada_table[layer]
        shift_m, scale_m, gate_m, shift_f, scale_f, gate_f = (mod[:, i:i + 1] for i in range(N_MOD))
        hm = _rownorm(xf, norm_mix[layer], scale=scale_m, shift=shift_m, rows_per_batch=seq)
        j = layer // 2
        if layer % 2 == 0:
            w_main, w_small, q_rank, kv_rank = _even_in_weights(e_w_in[j])
            main = _mm(hm, w_main, tn=768)
            krdt = _mm(hm, w_small, tn=LANES)
            o_cq = 2 * SSD_WIDTH + 2 * SSD_BC_WIDTH
            cqn = _rownorm(main, e_q_norm[j], col_block=o_cq // q_rank, width=q_rank)
            ckvn = _rownorm(main, e_kv_norm[j], col_block=(o_cq + q_rank) // kv_rank, width=kv_rank)
            q = _mm(cqn, _q_weights(e_w_uq[j]))
            kv = _mm(ckvn, _kv_weights(e_w_ukv[j]))
            o_attn = _attention(q, kv, krdt, cc, ss, bsz=bsz, seq=seq)
            y_ssd = _ssd(main, krdt, e_conv_w[j], e_conv_b[j], e_dt_bias[j], e_a_log[j],
                         e_d_skip[j], e_ssd_norm[j], bsz=bsz, seq=seq)
            xf = _mm_res([(y_ssd, e_w_out_b, 0), (o_attn, e_w_out_b, 1)], xf, gate_m, layer=j,
                         rows_per_batch=seq)
        else:
            gy = _mm(hm, o_w_y_b, layer=j, bias=o_b_y[j])
            ux = _mm(hm, o_w_x_b, layer=j, bias=o_b_x[j])
            y_lru = _rglru(gy, ux, j, o_conv_w[j], o_conv_b[j], o_w_a_b, o_b_a[j], o_w_i_b, o_b_i[j],
                           o_lam[j], bsz=bsz, seq=seq)
            r = y_lru.shape[1]
            xf = _mm_res([(y_lru, o_w_out_b, 0)], xf, gate_m, layer=j, bias=o_b_out[j],
                         rows_per_batch=seq, tk=r // 2)
        hf = _rownorm(xf, norm_ffn[layer], scale=scale_f, shift=shift_f, rows_per_batch=seq)
        if layer + 1 < depth:
            hid, next_up_b, next_down_b = _mm(hf, up_b, act="relu2", out_dtype=BF16,
                                              cast=[(ffn_up, layer + 1), (ffn_down, layer + 1)])
        else:
            hid = _mm(hf, up_b, act="relu2", out_dtype=BF16)
        xf = _mm_res([(hid, down_b, 0)], xf, gate_f, rows_per_batch=seq, tk=2048)
        if layer + 1 < depth:
            up_b, down_b = next_up_b, next_down_b

    out = _rownorm(xf, norm_final, out_dtype=F32)
    return out.reshape(bsz, seq, d)
```

```python
import functools

import jax
import jax.numpy as jnp
from jax import lax
from jax.experimental import pallas as pl
from jax.experimental.pallas import tpu as pltpu

F32 = jnp.float32
BF16 = jnp.bfloat16

EPS = 1e-6
ROPE_THETA = 10000.0
N_MOD = 6

SSD_HEAD_DIM = 64
SSD_GROUPS = 8
SSD_HEADS_PER_GROUP = 4
SSD_STATE = 128
SSD_CONV = 4
SSD_HEADS = SSD_GROUPS * SSD_HEADS_PER_GROUP
SSD_WIDTH = SSD_HEADS * SSD_HEAD_DIM
SSD_BC_WIDTH = SSD_GROUPS * SSD_STATE
SSD_TILE = 256

MLA_HEADS = 16
MLA_NOPE = 128
MLA_ROPE = 64
MLA_V_DIM = 128
MLA_STREAM_CHUNK = 64
MLA_Q_TILE = 256

LRU_BLOCK = 256
LRU_CONV = 4
LRU_C = 8.0
LRU_TILE = 256
LRU_COLS = 512

LANES = 128
SUBLANES = 8
DT_LANE0 = 64
VMEM_LIMIT_CAP = 60000 * 1024


def _vmem_limit(block_bytes, extra=0):
    need = 2 * block_bytes + extra + (4 << 20)
    return int(min(max(need, 16 << 20), VMEM_LIMIT_CAP))


def _params(semantics, vmem, flags=None):
    return pltpu.CompilerParams(dimension_semantics=semantics, vmem_limit_bytes=vmem, flags=flags)


def _nbytes(shape, dtype):
    n = 1
    for s in shape:
        n *= s
    return n * jnp.dtype(dtype).itemsize


def _silu(x):
    return x * jax.nn.sigmoid(x)


def _softplus(x):
    return jnp.maximum(x, 0.0) + jnp.log(1.0 + jnp.exp(-jnp.abs(x)))


def _split3(v):
    a1 = v.astype(BF16)
    r1 = v - a1.astype(F32)
    a2 = r1.astype(BF16)
    a3 = (r1 - a2.astype(F32)).astype(BF16)
    return a1, a2, a3


def _causal_conv(hist, blk, w, bias):
    rows, cols = blk.shape
    taps = w.shape[0]
    groups = rows // SUBLANES
    x3 = jnp.concatenate([hist, blk], axis=0).reshape(groups + 1, SUBLANES, cols)
    sub = lax.broadcasted_iota(jnp.int32, (groups, SUBLANES, cols), 1)
    y = bias + w[taps - 1:taps] * x3[1:]
    for s in range(1, taps):
        r = pltpu.roll(x3, s, 1)
        y = y + w[taps - 1 - s:taps - s] * jnp.where(sub >= s, r[1:], r[:-1])
    return y.reshape(rows, cols)


def _dot(a, b):
    return jnp.dot(a, b, preferred_element_type=F32)


def _dot_nt(a, b):
    return lax.dot_general(a, b, (((1,), (1,)), ((), ())), preferred_element_type=F32)


def _dot_tn(a, b):
    return lax.dot_general(a, b, (((0,), (0,)), ((), ())), preferred_element_type=F32)


def _select_dot(sel, v):
    a1, a2, a3 = _split3(v)
    return _dot(sel, a1) + _dot(sel, a2) + _dot(sel, a3)


def _dot_select(v, sel):
    a1, a2, a3 = _split3(v)
    return _dot(a1, sel) + _dot(a2, sel) + _dot(a3, sel)


def _ada_kernel(c_ref, w_ref, b_ref, o_ref):
    s = _silu(c_ref[...]).astype(BF16)
    o_ref[...] = _dot(s, w_ref[...].astype(BF16)) + b_ref[...]


def _ada_proj(c, w, b):
    bsz, d = c.shape
    n = w.shape[1]
    tn = 512
    blocks = _nbytes((d, tn), F32) + _nbytes((bsz, d), F32) + 2 * _nbytes((bsz, tn), F32)
    return pl.pallas_call(
        _ada_kernel,
        out_shape=jax.ShapeDtypeStruct((bsz, n), F32),
        grid=(n // tn,),
        in_specs=[pl.BlockSpec((bsz, d), lambda j: (0, 0)),
                  pl.BlockSpec((d, tn), lambda j: (0, j)),
                  pl.BlockSpec((1, tn), lambda j: (0, j))],
        out_specs=pl.BlockSpec((bsz, tn), lambda j: (0, j)),
        compiler_params=_params(("arbitrary",), _vmem_limit(blocks, _nbytes((d, tn), BF16))),
        name="ada_proj",
    )(c, w, b.reshape(1, n))


def _rope_table_kernel(pos_ref, freq_ref, cc_ref, ss_ref):
    ang = pos_ref[...].astype(F32) * freq_ref[...]
    lane = lax.broadcasted_iota(jnp.int32, ang.shape, 1)
    half = MLA_ROPE // 2
    c = jnp.cos(ang)
    s = jnp.sin(ang)
    cc_ref[...] = jnp.where(lane < MLA_ROPE, c, 0.0)
    ss_ref[...] = jnp.where(lane < half, -s, jnp.where(lane < MLA_ROPE, s, 0.0))


def _rope_tables(positions):
    t = positions.size
    half = MLA_ROPE // 2
    inv_freq = ROPE_THETA ** (-jnp.arange(0, MLA_ROPE, 2, dtype=F32) / MLA_ROPE)
    freq = jnp.concatenate([inv_freq, inv_freq, jnp.zeros((LANES - 2 * half,), F32)]).reshape(1, LANES)
    tm = min(t, 2048)
    return pl.pallas_call(
        _rope_table_kernel,
        out_shape=(jax.ShapeDtypeStruct((t, LANES), F32), jax.ShapeDtypeStruct((t, LANES), F32)),
        grid=(t // tm,),
        in_specs=[pl.BlockSpec((tm, 1), lambda i: (i, 0)),
                  pl.BlockSpec((1, LANES), lambda i: (0, 0))],
        out_specs=(pl.BlockSpec((tm, LANES), lambda i: (i, 0)),
                   pl.BlockSpec((tm, LANES), lambda i: (i, 0))),
        compiler_params=_params(("arbitrary",), _vmem_limit(4 * _nbytes((tm, LANES), F32))),
        name="rope_tables",
    )(positions.reshape(t, 1), freq)


def _rownorm_kernel(x_ref, g_ref, *rest, modulate):
    o_ref = rest[-1]
    x = x_ref[...]
    y = x * lax.rsqrt(jnp.mean(x * x, axis=-1, keepdims=True) + EPS) * g_ref[...]
    if modulate:
        sc_ref, sh_ref = rest[0], rest[1]
        y = y * (1.0 + sc_ref[...]) + sh_ref[...]
    o_ref[...] = y.astype(o_ref.dtype)


def _rownorm(x, gain, *, col_block=0, width=None, scale=None, shift=None, rows_per_batch=None,
             out_dtype=BF16):
    t = x.shape[0]
    width = x.shape[1] if width is None else width
    tm = min(512, t)
    modulate = scale is not None
    in_specs = [pl.BlockSpec((tm, width), lambda i: (i, col_block)),
                pl.BlockSpec((1, width), lambda i: (0, 0))]
    args = [x, gain.reshape(1, width)]
    if modulate:
        bmap = lambda i: ((i * tm) // rows_per_batch, 0, 0)
        in_specs += [pl.BlockSpec((None, 1, width), bmap), pl.BlockSpec((None, 1, width), bmap)]
        args += [scale, shift]
    blocks = _nbytes((tm, width), F32) + _nbytes((tm, width), out_dtype) + 3 * _nbytes((1, width), F32)
    return pl.pallas_call(
        functools.partial(_rownorm_kernel, modulate=modulate),
        out_shape=jax.ShapeDtypeStruct((t, width), out_dtype),
        grid=(t // tm,),
        in_specs=in_specs,
        out_specs=pl.BlockSpec((tm, width), lambda i: (i, 0)),
        compiler_params=_params(("arbitrary",), _vmem_limit(blocks, 2 * _nbytes((tm, width), F32))),
        name="rownorm",
    )(*args)


def _mm_kernel(x_ref, w_ref, *rest, act, has_bias, n_cast):
    pos = int(has_bias)
    cast_in = rest[pos:pos + n_cast]
    o_ref = rest[pos + n_cast]
    cast_out = rest[pos + n_cast + 1:]
    acc = _dot(x_ref[...], w_ref[...])
    if has_bias:
        acc = acc + rest[0][...]
    if act == "relu2":
        r = jnp.maximum(acc, 0.0)
        acc = r * r
    o_ref[...] = acc.astype(o_ref.dtype)
    for src, dst in zip(cast_in, cast_out):
        dst[...] = src[...].astype(dst.dtype)


def _weight_spec(w, layer, rows, cols, index):
    if w.ndim == 2:
        return pl.BlockSpec((rows, cols), index)
    return pl.BlockSpec((None, rows, cols), lambda *g: (layer,) + tuple(index(*g)))


BF16_ROWS = 16


def _cast_slab(rows, steps):
    slab = BF16_ROWS
    while rows % slab or rows // slab > steps:
        slab += BF16_ROWS
    return slab


def _mm(x, w, *, layer=None, bias=None, act=None, out_dtype=F32, tm=1024, tn=1024, cast=()):
    m, k = x.shape
    n = w.shape[-1]
    tm = min(tm, m)
    tn = min(tn, n)
    gj = n // tn
    steps = (m // tm) * gj
    has_bias = bias is not None
    in_specs = [pl.BlockSpec((tm, k), lambda i, j: (i, 0)),
                _weight_spec(w, layer, k, tn, lambda i, j: (0, j))]
    args = [x, w]
    if has_bias:
        in_specs.append(pl.BlockSpec((1, tn), lambda i, j: (0, j)))
        args.append(bias.reshape(1, n))
    blocks = _nbytes((tm, k), x.dtype) + _nbytes((k, tn), w.dtype) + _nbytes((tm, tn), out_dtype)
    out_shape = [jax.ShapeDtypeStruct((m, n), out_dtype)]
    out_specs = [pl.BlockSpec((tm, tn), lambda i, j: (i, j))]
    for src, src_layer in cast:
        _, rows, cols = src.shape
        slab = _cast_slab(rows, steps)
        n_slabs = rows // slab
        step = lambda i, j, n_slabs=n_slabs: jnp.minimum(i * gj + j, n_slabs - 1)
        in_specs.append(pl.BlockSpec((None, slab, cols),
                                     lambda i, j, src_layer=src_layer, step=step: (src_layer, step(i, j), 0)))
        args.append(src)
        out_shape.append(jax.ShapeDtypeStruct((rows, cols), BF16))
        out_specs.append(pl.BlockSpec((slab, cols), lambda i, j, step=step: (step(i, j), 0)))
        blocks += _nbytes((slab, cols), F32) + _nbytes((slab, cols), BF16)
    outs = pl.pallas_call(
        functools.partial(_mm_kernel, act=act, has_bias=has_bias, n_cast=len(cast)),
        out_shape=out_shape,
        grid=(m // tm, gj),
        in_specs=in_specs,
        out_specs=out_specs,
        compiler_params=_params(("arbitrary", "arbitrary"),
                                _vmem_limit(blocks, 3 * _nbytes((tm, tn), F32))),
        name="mm",
    )(*args)
    return outs if cast else outs[0]


def _mm_res_kernel(*refs, n_pairs, has_bias, nk):
    xs = refs[0:2 * n_pairs:2]
    ws = refs[1:2 * n_pairs:2]
    pos = 2 * n_pairs
    res_ref, gate_ref = refs[pos], refs[pos + 1]
    pos += 2
    bias_ref = refs[pos] if has_bias else None
    pos += int(has_bias)
    o_ref = refs[pos]

    def partial_sum():
        acc = _dot(xs[0][...], ws[0][...])
        for x_ref, w_ref in zip(xs[1:], ws[1:]):
            acc = acc + _dot(x_ref[...], w_ref[...])
        return acc

    def finish(acc):
        if has_bias:
            acc = acc + bias_ref[...]
        o_ref[...] = res_ref[...] + gate_ref[...] * acc

    if nk == 1:
        finish(partial_sum())
        return

    kk = pl.program_id(2)

    @pl.when(kk == 0)
    def _():
        o_ref[...] = partial_sum()

    @pl.when((kk > 0) & (kk < nk - 1))
    def _():
        o_ref[...] += partial_sum()

    @pl.when(kk == nk - 1)
    def _():
        finish(o_ref[...] + partial_sum())


def _mm_res(pairs, res, gate, *, layer=None, bias=None, rows_per_batch, tm=1024, tn=1024, tk=None):
    m, n = res.shape
    k = pairs[0][0].shape[1]
    tk = k if tk is None else tk
    nk = k // tk
    tm = min(tm, rows_per_batch)
    has_bias = bias is not None
    in_specs, args = [], []
    blocks = 0
    for x, w, row0 in pairs:
        in_specs += [pl.BlockSpec((tm, tk), lambda i, j, kk: (i, kk)),
                     _weight_spec(w, layer, tk, tn, lambda i, j, kk, row0=row0: (row0 + kk, j))]
        args += [x, w]
        blocks += _nbytes((tm, tk), x.dtype) + _nbytes((tk, tn), w.dtype)
    in_specs += [pl.BlockSpec((tm, tn), lambda i, j, kk: (i, j)),
                 pl.BlockSpec((None, 1, tn), lambda i, j, kk: ((i * tm) // rows_per_batch, 0, j))]
    args += [res, gate]
    blocks += 2 * _nbytes((tm, tn), F32)
    if has_bias:
        in_specs.append(pl.BlockSpec((1, tn), lambda i, j, kk: (0, j)))
        args.append(bias.reshape(1, n))
    return pl.pallas_call(
        functools.partial(_mm_res_kernel, n_pairs=len(pairs), has_bias=has_bias, nk=nk),
        out_shape=jax.ShapeDtypeStruct((m, n), F32),
        grid=(m // tm, n // tn, nk),
        in_specs=in_specs,
        out_specs=pl.BlockSpec((tm, tn), lambda i, j, kk: (i, j)),
        compiler_params=_params(("arbitrary", "arbitrary", "arbitrary"),
                                _vmem_limit(blocks, 4 * _nbytes((tm, tn), F32))),
        name="mm_res",
    )(*args)


def _attn_kernel(qn_ref, qr_ref, kn_ref, v_ref, kr_ref, cc_ref, ss_ref, o_ref, *, scale, tq):
    seq = qn_ref.shape[0]
    cc = cc_ref[...]
    ss = ss_ref[...]
    lane = lax.broadcasted_iota(jnp.int32, (seq, LANES), 1)
    half = MLA_ROPE // 2

    def rope(x):
        swapped = jnp.where(lane < half, pltpu.roll(x, LANES - half, 1), pltpu.roll(x, half, 1))
        return x * cc + swapped * ss

    qf = jnp.concatenate([(qn_ref[...] * scale).astype(BF16),
                          (rope(qr_ref[...]) * scale).astype(BF16)], axis=-1)
    kf = jnp.concatenate([kn_ref[...].astype(BF16), rope(kr_ref[...]).astype(BF16)], axis=-1)
    v = v_ref[...].astype(BF16)

    r_chunk = lax.broadcasted_iota(jnp.int32, (tq, tq), 0) // MLA_STREAM_CHUNK
    c_chunk = lax.broadcasted_iota(jnp.int32, (tq, tq), 1) // MLA_STREAM_CHUNK
    diag_ok = c_chunk <= r_chunk

    for i in range(seq // tq):
        q0, q1 = i * tq, (i + 1) * tq
        s_d = jnp.where(diag_ok, _dot_nt(qf[q0:q1], kf[q0:q1]), -jnp.inf)
        m = jnp.max(s_d, axis=-1, keepdims=True)
        if i > 0:
            s_o = _dot_nt(qf[q0:q1], kf[:q0])
            m = jnp.maximum(m, jnp.max(s_o, axis=-1, keepdims=True))
        p_d = jnp.exp(s_d - m)
        denom = jnp.sum(p_d, axis=-1, keepdims=True)
        acc = _dot(p_d.astype(BF16), v[q0:q1])
        if i > 0:
            p_o = jnp.exp(s_o - m)
            denom = denom + jnp.sum(p_o, axis=-1, keepdims=True)
            acc = acc + _dot(p_o.astype(BF16), v[:q0])
        o_ref[q0:q1, :] = (acc / denom).astype(o_ref.dtype)


def _attention(q, kv, krdt, cc, ss, *, bsz, seq):
    t = q.shape[0]
    scale = (MLA_NOPE + MLA_ROPE) ** -0.5
    blk = lambda off: pl.BlockSpec((seq, LANES), lambda b, h: (b, off + h))
    tok = pl.BlockSpec((seq, LANES), lambda b, h: (b, 0))
    blocks = 7 * _nbytes((seq, LANES), F32) + _nbytes((seq, LANES), BF16)
    return pl.pallas_call(
        functools.partial(_attn_kernel, scale=scale, tq=MLA_Q_TILE),
        out_shape=jax.ShapeDtypeStruct((t, MLA_HEADS * MLA_V_DIM), BF16),
        grid=(bsz, MLA_HEADS),
        in_specs=[blk(0), blk(MLA_HEADS), blk(0), blk(MLA_HEADS), tok, tok, tok],
        out_specs=pl.BlockSpec((seq, MLA_V_DIM), lambda b, h: (b, h)),
        compiler_params=_params(("arbitrary", "arbitrary"), _vmem_limit(blocks, 16 << 20)),
        name="mla_attention",
    )(q, q, kv, kv, krdt, cc, ss)


def _ssd_kernel(z_ref, x_ref, bc_ref, dk_ref, cwx_ref, cbx_ref, cwbc_ref, cbbc_ref,
                dtb_ref, alog_ref, dskip_ref, gain_ref, o_ref,
                tailx_ref, tailbc_ref, state_ref):
    q = x_ref.shape[0]
    p = SSD_HEAD_DIM
    gw = SSD_HEADS_PER_GROUP * p
    n = SSD_STATE

    @pl.when(pl.program_id(1) == 0)
    def _():
        tailx_ref[...] = jnp.zeros_like(tailx_ref)
        tailbc_ref[...] = jnp.zeros_like(tailbc_ref)
        state_ref[...] = jnp.zeros_like(state_ref)

    def conv_silu(tail_ref, blk_ref, w_ref, b_ref):
        blk = blk_ref[...]
        y = _causal_conv(tail_ref[...], blk, w_ref[...], b_ref[...])
        tail_ref[...] = blk[q - SUBLANES:]
        return _silu(y)

    xc = conv_silu(tailx_ref, x_ref, cwx_ref, cbx_ref)
    bcc = conv_silu(tailbc_ref, bc_ref, cwbc_ref, cbbc_ref)

    lane = lax.broadcasted_iota(jnp.int32, (q, LANES), 1)
    lane1 = lax.broadcasted_iota(jnp.int32, (1, LANES), 1)
    is_dt = (lane1 >= DT_LANE0) & (lane1 < DT_LANE0 + SSD_HEADS)
    dt = _softplus(dk_ref[...] + dtb_ref[...])
    a_row = jnp.where(is_dt, -jnp.exp(alog_ref[...]), 0.0)
    ad = dt * a_row

    rr = lax.broadcasted_iota(jnp.int32, (q, q), 0)
    cc_ = lax.broadcasted_iota(jnp.int32, (q, q), 1)
    causal = cc_ <= rr
    tri = jnp.where(causal, 1.0, 0.0).astype(BF16)
    a_cs = _select_dot(tri, ad)
    a_cs_t = a_cs.T

    er = lax.broadcasted_iota(jnp.int32, (LANES, SSD_WIDTH), 0)
    ec = lax.broadcasted_iota(jnp.int32, (LANES, SSD_WIDTH), 1)
    expand = jnp.where(er - DT_LANE0 == ec // p, 1.0, 0.0).astype(BF16)
    dt_x = _dot_select(dt, expand)
    acs_x = _dot_select(a_cs, expand)
    dec_x = jnp.exp(acs_x)
    aend_x = acs_x[q - 1:q, :]
    xs = xc * dt_x
    xs_end = xs * jnp.exp(aend_x - acs_x)
    chunk_decay = dec_x[q - 1:q, :]

    lo_half = lane < p
    zs = _silu(z_ref[...])
    dskip = dskip_ref[...]
    gain = gain_ref[...]

    for g in range(SSD_GROUPS):
        gs = slice(g * gw, (g + 1) * gw)
        b_g = bcc[:, g * n:(g + 1) * n].astype(BF16)
        c_g = bcc[:, SSD_BC_WIDTH + g * n:SSD_BC_WIDTH + (g + 1) * n].astype(BF16)
        cb = _dot_nt(c_g, b_g)
        s_prev = state_ref[g]
        y_off = _dot(c_g, s_prev.astype(BF16)) * dec_x[:, gs]
        s_new = _dot_tn(b_g, xs_end[:, gs].astype(BF16))
        state_ref[g] = chunk_decay[:, gs] * s_prev + s_new

        pair_out = []
        for pr in range(SSD_HEADS_PER_GROUP // 2):
            ps = slice(g * gw + pr * LANES, g * gw + (pr + 1) * LANES)
            xs_pair = xs[:, ps]
            acc = None
            for hh in range(2):
                h = g * SSD_HEADS_PER_GROUP + pr * 2 + hh
                col = jnp.sum(jnp.where(lane == DT_LANE0 + h, a_cs, 0.0), axis=-1, keepdims=True)
                row = a_cs_t[DT_LANE0 + h:DT_LANE0 + h + 1, :]
                seg = jnp.exp(jnp.where(causal, col - row, -jnp.inf))
                mh = (cb * seg).astype(BF16)
                keep = lo_half if hh == 0 else jnp.logical_not(lo_half)
                term = _dot(mh, jnp.where(keep, xs_pair, 0.0).astype(BF16))
                acc = term if acc is None else acc + term
            pair_out.append(acc)
        y = jnp.concatenate(pair_out, axis=-1) + y_off + dskip[:, gs] * xc[:, gs]
        y = y * zs[:, gs]
        y = y * lax.rsqrt(jnp.mean(y * y, axis=-1, keepdims=True) + EPS) * gain[:, gs]
        o_ref[:, gs] = y.astype(o_ref.dtype)


def _ssd(main, krdt, conv_w, conv_b, dt_bias, a_log, d_skip, ssd_norm, *, bsz, seq):
    t = main.shape[0]
    q = SSD_TILE
    steps = seq // q
    w = SSD_WIDTH
    row = lambda off: pl.BlockSpec((q, w), lambda b, c: (b * steps + c, off))
    full = lambda r: pl.BlockSpec((r, w), lambda b, c: (0, 0))
    lane_row = pl.BlockSpec((1, LANES), lambda b, c: (0, 0))
    pad = lambda v: jnp.zeros((1, LANES), F32).at[0, DT_LANE0:DT_LANE0 + SSD_HEADS].set(v)
    blocks = 3 * _nbytes((q, w), F32) + _nbytes((q, LANES), F32) + _nbytes((q, w), BF16) \
        + 12 * _nbytes((1, w), F32)
    scratch_bytes = 2 * _nbytes((SUBLANES, w), F32) + _nbytes((SSD_GROUPS, SSD_STATE, w // SSD_GROUPS), F32)
    return pl.pallas_call(
        _ssd_kernel,
        out_shape=jax.ShapeDtypeStruct((t, w), BF16),
        grid=(bsz, steps),
        in_specs=[row(0), row(1), row(2),
                  pl.BlockSpec((q, LANES), lambda b, c: (b * steps + c, 0)),
                  full(SSD_CONV), full(1), full(SSD_CONV), full(1),
                  lane_row, lane_row, full(1), full(1)],
        out_specs=pl.BlockSpec((q, w), lambda b, c: (b * steps + c, 0)),
        scratch_shapes=[pltpu.VMEM((SUBLANES, w), F32), pltpu.VMEM((SUBLANES, w), F32),
                        pltpu.VMEM((SSD_GROUPS, SSD_STATE, w // SSD_GROUPS), F32)],
        compiler_params=_params(("arbitrary", "arbitrary"),
                                _vmem_limit(blocks, scratch_bytes + (24 << 20))),
        name="ssd_mixer",
    )(main, main, main, krdt,
      conv_w[:, :w], conv_b[:w].reshape(1, w), conv_w[:, w:], conv_b[w:].reshape(1, w),
      pad(dt_bias), pad(a_log),
      jnp.repeat(d_skip, SSD_HEAD_DIM).reshape(1, w), ssd_norm.reshape(1, w))


def _rglru_kernel(x_ref, wy_ref, by_ref, wx_ref, bx_ref, cw_ref, cb_ref, wa_ref, ba_ref,
                  wi_ref, bi_ref, lam_ref, o_ref, tail_ref, h_ref, ux_buf, gy_buf, *, tiles_per_batch):
    i = pl.program_id(0)
    j = pl.program_id(1)
    tm, cols = o_ref.shape
    tl = LRU_TILE
    groups = tl // SUBLANES
    nblk = cols // LRU_BLOCK
    w = cw_ref[...]
    cb = cb_ref[...]
    ba = ba_ref[...]
    bi = bi_ref[...]
    by = by_ref[...]
    bx = bx_ref[...]
    neg_c_sp = -LRU_C * _softplus(-lam_ref[...])
    sub = lax.broadcasted_iota(jnp.int32, (groups, SUBLANES, cols), 1)

    @pl.when(i % tiles_per_batch == 0)
    def _():
        tail_ref[j] = jnp.zeros((SUBLANES, cols), F32)
        h_ref[j] = jnp.zeros((SUBLANES, cols), F32)

    hist = tail_ref[j]
    h_in = h_ref[j]
    def project(s):
        xs = x_ref[s * tl:(s + 1) * tl, :]
        ux_buf[s % 2] = _dot(xs, wx_ref[...]) + bx
        gy_buf[s % 2] = _dot(xs, wy_ref[...]) + by

    def conv(s, hist):
        blk = ux_buf[s % 2]
        return _causal_conv(hist, blk, w, cb), blk[tl - SUBLANES:]

    def gate_logits(u):
        ub = u.astype(BF16)
        ra, ia = [], []
        for bk in range(nblk):
            cs = slice(bk * LRU_BLOCK, (bk + 1) * LRU_BLOCK)
            ra.append(_dot(ub[:, cs], wa_ref[bk]))
            ia.append(_dot(ub[:, cs], wi_ref[bk]))
        return jnp.concatenate(ra, axis=-1) + ba, jnp.concatenate(ia, axis=-1) + bi

    n_sub = tm // tl
    project(0)
    u, hist = conv(0, hist)
    for s in range(n_sub):
        rows = slice(s * tl, (s + 1) * tl)
        r_logit, i_logit = gate_logits(u)
        if s + 1 < n_sub:
            project(s + 1)
        gy = gy_buf[s % 2]
        r_gate = jax.nn.sigmoid(r_logit)
        i_gate = jax.nn.sigmoid(i_logit)
        a2 = jnp.exp(neg_c_sp * r_gate)
        b2 = jnp.sqrt(1.0 - a2 * a2) * (i_gate * u)

        a = a2.reshape(groups, SUBLANES, cols)
        b = b2.reshape(groups, SUBLANES, cols)
        for d in (1, 2, 4):
            ok = sub >= d
            a_sh = jnp.where(ok, pltpu.roll(a, d, 1), 1.0)
            b_sh = jnp.where(ok, pltpu.roll(b, d, 1), 0.0)
            b = b + a * b_sh
            a = a * a_sh

        outs = []
        for r in range(groups):
            h_blk = b[r] + a[r] * h_in
            outs.append(h_blk)
            h_in = jnp.broadcast_to(h_blk[SUBLANES - 1:SUBLANES], (SUBLANES, cols))
        hs = jnp.concatenate(outs, axis=0)
        o_ref[rows, :] = (hs * jax.nn.gelu(gy)).astype(o_ref.dtype)
        if s + 1 < n_sub:
            u, hist = conv(s + 1, hist)
    tail_ref[j] = hist
    h_ref[j] = h_in


def _rglru(hm, layer, w_y, b_y, w_x, b_x, conv_w, conv_b, w_a, b_a, w_i, b_i, lam, *, seq):
    t, d = hm.shape
    r = w_x.shape[-1]
    cols = LRU_COLS
    nblk = cols // LRU_BLOCK
    ncb = r // cols
    tm = min(1024, seq)
    vec = lambda: pl.BlockSpec((1, cols), lambda i, j: (0, j))
    proj = pl.BlockSpec((d, cols), lambda i, j: (0, j))
    wblk = pl.BlockSpec((None, nblk, LRU_BLOCK, LRU_BLOCK), lambda i, j: (layer, j, 0, 0))
    blocks = _nbytes((tm, d), BF16) + 2 * _nbytes((d, cols), BF16) + _nbytes((tm, cols), BF16) \
        + 2 * _nbytes((nblk, LRU_BLOCK, LRU_BLOCK), BF16) + 10 * _nbytes((1, cols), F32)
    scratch_bytes = 2 * _nbytes((ncb, SUBLANES, cols), F32) + 4 * _nbytes((LRU_TILE, cols), F32)
    return pl.pallas_call(
        functools.partial(_rglru_kernel, tiles_per_batch=seq // tm),
        out_shape=jax.ShapeDtypeStruct((t, r), BF16),
        grid=(t // tm, ncb),
        in_specs=[pl.BlockSpec((tm, d), lambda i, j: (i, 0)),
                  proj, vec(), proj, vec(),
                  pl.BlockSpec((LRU_CONV, cols), lambda i, j: (0, j)), vec(),
                  wblk, vec(), wblk, vec(), vec()],
        out_specs=pl.BlockSpec((tm, cols), lambda i, j: (i, j)),
        scratch_shapes=[pltpu.VMEM((ncb, SUBLANES, cols), F32), pltpu.VMEM((ncb, SUBLANES, cols), F32),
                        pltpu.VMEM((2, LRU_TILE, cols), F32), pltpu.VMEM((2, LRU_TILE, cols), F32)],
        compiler_params=_params(("arbitrary", "arbitrary"),
                                _vmem_limit(blocks, scratch_bytes + (16 << 20))),
        name="rglru_mixer",
    )(hm, w_y, b_y.reshape(1, r), w_x, b_x.reshape(1, r), conv_w, conv_b.reshape(1, r),
      w_a, b_a.reshape(1, r), w_i, b_i.reshape(1, r), lam.reshape(1, r))


def _even_in_weights(w_in):
    d = w_in.shape[0]
    o_z, o_xbc = 0, SSD_WIDTH
    o_dt = o_xbc + SSD_WIDTH + 2 * SSD_BC_WIDTH
    o_cq = o_dt + SSD_HEADS
    q_rank = d // 4
    kv_rank = d // 8
    o_ckv = o_cq + q_rank
    o_kr = o_ckv + kv_rank
    w_main = jnp.concatenate([w_in[:, o_z:o_dt], w_in[:, o_cq:o_kr]], axis=1).astype(BF16)
    w_small = jnp.concatenate(
        [w_in[:, o_kr:o_kr + MLA_ROPE], w_in[:, o_dt:o_cq],
         jnp.zeros((d, LANES - MLA_ROPE - SSD_HEADS), w_in.dtype)], axis=1).astype(BF16)
    return w_main, w_small, q_rank, kv_rank


def _q_weights(w_uq):
    r = w_uq.shape[0]
    w = w_uq.reshape(r, MLA_HEADS, MLA_NOPE + MLA_ROPE)
    nope = w[:, :, :MLA_NOPE].reshape(r, MLA_HEADS * MLA_NOPE)
    rope = jnp.pad(w[:, :, MLA_NOPE:], ((0, 0), (0, 0), (0, LANES - MLA_ROPE)))
    return jnp.concatenate([nope, rope.reshape(r, MLA_HEADS * LANES)], axis=1).astype(BF16)


def _kv_weights(w_ukv):
    r = w_ukv.shape[0]
    w = w_ukv.reshape(r, MLA_HEADS, MLA_NOPE + MLA_V_DIM)
    k = w[:, :, :MLA_NOPE].reshape(r, MLA_HEADS * MLA_NOPE)
    v = w[:, :, MLA_NOPE:].reshape(r, MLA_HEADS * MLA_V_DIM)
    return jnp.concatenate([k, v], axis=1).astype(BF16)


def kernel(x, c, positions, ada_w, ada_b, ada_table, norm_mix, norm_ffn, norm_final, ffn_up, ffn_down, e_w_in, e_conv_w, e_conv_b, e_dt_bias, e_a_log, e_d_skip, e_ssd_norm, e_q_norm, e_w_uq, e_kv_norm, e_w_ukv, e_w_out, o_w_y, o_b_y, o_w_x, o_b_x, o_conv_w, o_conv_b, o_w_a, o_b_a, o_w_i, o_b_i, o_lam, o_w_out, o_b_out):
    bsz, seq, d = x.shape
    t = bsz * seq
    depth = ada_table.shape[0]
    xf = x.reshape(t, d)

    cc, ss = _rope_tables(positions)
    mod_all = _ada_proj(c, ada_w, ada_b).reshape(bsz, N_MOD, d)

    o_w_a_b, o_w_i_b = o_w_a.astype(BF16), o_w_i.astype(BF16)

    def mixer_casts(layer):
        jj = layer // 2
        if layer % 2 == 0:
            return {"e_out": (e_w_out, jj)}
        return {"o_y": (o_w_y, jj), "o_x": (o_w_x, jj), "o_out": (o_w_out, jj)}

    def mm_casting(xin, w, wanted, **kw):
        names = list(wanted)
        outs = _mm(xin, w, cast=[wanted[n] for n in names], **kw)
        return outs[0], dict(zip(names, outs[1:]))

    wb = {}

    for layer in range(depth):
        mod = mod_all + ada_table[layer]
        shift_m, scale_m, gate_m, shift_f, scale_f, gate_f = (mod[:, i:i + 1] for i in range(N_MOD))
        hm = _rownorm(xf, norm_mix[layer], scale=scale_m, shift=shift_m, rows_per_batch=seq)
        j = layer // 2
        if layer % 2 == 0:
            w_main, w_small, q_rank, kv_rank = _even_in_weights(e_w_in[j])
            if layer == 0:
                main, first = mm_casting(hm, w_main, {"up": (ffn_up, 0), **mixer_casts(0)}, tn=768)
                wb.update(first)
            else:
                main = _mm(hm, w_main, tn=768)
            krdt = _mm(hm, w_small, tn=LANES)
            o_cq = 2 * SSD_WIDTH + 2 * SSD_BC_WIDTH
            cqn = _rownorm(main, e_q_norm[j], col_block=o_cq // q_rank, width=q_rank)
            ckvn = _rownorm(main, e_kv_norm[j], col_block=(o_cq + q_rank) // kv_rank, width=kv_rank)
            q = _mm(cqn, _q_weights(e_w_uq[j]))
            kv = _mm(ckvn, _kv_weights(e_w_ukv[j]))
            o_attn = _attention(q, kv, krdt, cc, ss, bsz=bsz, seq=seq)
            y_ssd = _ssd(main, krdt, e_conv_w[j], e_conv_b[j], e_dt_bias[j], e_a_log[j],
                         e_d_skip[j], e_ssd_norm[j], bsz=bsz, seq=seq)
            xf = _mm_res([(y_ssd, wb["e_out"], 0), (o_attn, wb["e_out"], 1)], xf, gate_m,
                         rows_per_batch=seq)
        else:
            y_lru = _rglru(hm, j, wb["o_y"], o_b_y[j], wb["o_x"], o_b_x[j], o_conv_w[j], o_conv_b[j],
                           o_w_a_b, o_b_a[j], o_w_i_b, o_b_i[j], o_lam[j], seq=seq)
            r = y_lru.shape[1]
            xf = _mm_res([(y_lru, wb["o_out"], 0)], xf, gate_m, bias=o_b_out[j],
                         rows_per_batch=seq, tk=r // 2)
        hf = _rownorm(xf, norm_ffn[layer], scale=scale_f, shift=shift_f, rows_per_batch=seq)
        wanted = {"down": (ffn_down, layer)}
        if layer + 1 < depth:
            wanted.update({"up": (ffn_up, layer + 1), **mixer_casts(layer + 1)})
        hid, nxt = mm_casting(hf, wb["up"], wanted, act="relu2", out_dtype=BF16)
        wb.update(nxt)
        xf = _mm_res([(hid, wb["down"], 0)], xf, gate_f, rows_per_batch=seq, tk=4096)

    out = _rownorm(xf, norm_final, out_dtype=F32)
    return out.reshape(bsz, seq, d)
```

```python
import functools
import math

import jax
import jax.numpy as jnp
from jax import lax
from jax.experimental import pallas as pl
from jax.experimental.pallas import tpu as pltpu

F32 = jnp.float32
BF16 = jnp.bfloat16

EPS = 1e-6
LOG2_E = math.log2(math.e)
ROPE_THETA = 10000.0
N_MOD = 6

SSD_HEAD_DIM = 64
SSD_GROUPS = 8
SSD_HEADS_PER_GROUP = 4
SSD_STATE = 128
SSD_CONV = 4
SSD_HEADS = SSD_GROUPS * SSD_HEADS_PER_GROUP
SSD_WIDTH = SSD_HEADS * SSD_HEAD_DIM
SSD_BC_WIDTH = SSD_GROUPS * SSD_STATE
SSD_TILE = 256

MLA_HEADS = 16
MLA_NOPE = 128
MLA_ROPE = 64
MLA_V_DIM = 128
MLA_STREAM_CHUNK = 64
MLA_Q_TILE = 256

LRU_BLOCK = 256
LRU_CONV = 4
LRU_C = 8.0
LRU_TILE = 256
LRU_COLS = 512
LRU_SLOTS = 3

LANES = 128
SUBLANES = 8
DT_LANE0 = 64
VMEM_LIMIT_CAP = 60000 * 1024


def _vmem_limit(block_bytes, extra=0):
    need = 2 * block_bytes + extra + (4 << 20)
    return int(min(max(need, 16 << 20), VMEM_LIMIT_CAP))


def _params(semantics, vmem, flags=None):
    return pltpu.CompilerParams(dimension_semantics=semantics, vmem_limit_bytes=vmem, flags=flags)


def _nbytes(shape, dtype):
    n = 1
    for s in shape:
        n *= s
    return n * jnp.dtype(dtype).itemsize


def _silu(x):
    return x * jax.nn.sigmoid(x)


def _softplus(x):
    return jnp.maximum(x, 0.0) + jnp.log(1.0 + jnp.exp(-jnp.abs(x)))


def _split3(v):
    a1 = v.astype(BF16)
    r1 = v - a1.astype(F32)
    a2 = r1.astype(BF16)
    a3 = (r1 - a2.astype(F32)).astype(BF16)
    return a1, a2, a3


def _causal_conv(hist, blk, w, bias):
    rows, cols = blk.shape
    taps = w.shape[0]
    groups = rows // SUBLANES
    x3 = jnp.concatenate([hist, blk], axis=0).reshape(groups + 1, SUBLANES, cols)
    sub = lax.broadcasted_iota(jnp.int32, (groups, SUBLANES, cols), 1)
    y = bias + w[taps - 1:taps] * x3[1:]
    for s in range(1, taps):
        r = pltpu.roll(x3, s, 1)
        y = y + w[taps - 1 - s:taps - s] * jnp.where(sub >= s, r[1:], r[:-1])
    return y.reshape(rows, cols)


def _dot(a, b):
    return jnp.dot(a, b, preferred_element_type=F32)


def _dot_nt(a, b):
    return lax.dot_general(a, b, (((1,), (1,)), ((), ())), preferred_element_type=F32)


def _dot_tn(a, b):
    return lax.dot_general(a, b, (((0,), (0,)), ((), ())), preferred_element_type=F32)


def _select_dot(sel, v):
    a1, a2, a3 = _split3(v)
    return _dot(sel, a1) + _dot(sel, a2) + _dot(sel, a3)


def _dot_select(v, sel):
    a1, a2, a3 = _split3(v)
    return _dot(a1, sel) + _dot(a2, sel) + _dot(a3, sel)


def _ada_kernel(c_ref, w_ref, b_ref, o_ref):
    s = _silu(c_ref[...]).astype(BF16)
    o_ref[...] = _dot(s, w_ref[...].astype(BF16)) + b_ref[...]


def _ada_proj(c, w, b):
    bsz, d = c.shape
    n = w.shape[1]
    tn = 512
    blocks = _nbytes((d, tn), F32) + _nbytes((bsz, d), F32) + 2 * _nbytes((bsz, tn), F32)
    return pl.pallas_call(
        _ada_kernel,
        out_shape=jax.ShapeDtypeStruct((bsz, n), F32),
        grid=(n // tn,),
        in_specs=[pl.BlockSpec((bsz, d), lambda j: (0, 0)),
                  pl.BlockSpec((d, tn), lambda j: (0, j)),
                  pl.BlockSpec((1, tn), lambda j: (0, j))],
        out_specs=pl.BlockSpec((bsz, tn), lambda j: (0, j)),
        compiler_params=_params(("arbitrary",), _vmem_limit(blocks, _nbytes((d, tn), BF16))),
        name="ada_proj",
    )(c, w, b.reshape(1, n))


def _rope_table_kernel(pos_ref, freq_ref, cc_ref, ss_ref):
    ang = pos_ref[...].astype(F32) * freq_ref[...]
    lane = lax.broadcasted_iota(jnp.int32, ang.shape, 1)
    half = MLA_ROPE // 2
    c = jnp.cos(ang)
    s = jnp.sin(ang)
    cc_ref[...] = jnp.where(lane < MLA_ROPE, c, 0.0)
    ss_ref[...] = jnp.where(lane < half, -s, jnp.where(lane < MLA_ROPE, s, 0.0))


def _rope_tables(positions):
    t = positions.size
    half = MLA_ROPE // 2
    inv_freq = ROPE_THETA ** (-jnp.arange(0, MLA_ROPE, 2, dtype=F32) / MLA_ROPE)
    freq = jnp.concatenate([inv_freq, inv_freq, jnp.zeros((LANES - 2 * half,), F32)]).reshape(1, LANES)
    tm = min(t, 2048)
    return pl.pallas_call(
        _rope_table_kernel,
        out_shape=(jax.ShapeDtypeStruct((t, LANES), F32), jax.ShapeDtypeStruct((t, LANES), F32)),
        grid=(t // tm,),
        in_specs=[pl.BlockSpec((tm, 1), lambda i: (i, 0)),
                  pl.BlockSpec((1, LANES), lambda i: (0, 0))],
        out_specs=(pl.BlockSpec((tm, LANES), lambda i: (i, 0)),
                   pl.BlockSpec((tm, LANES), lambda i: (i, 0))),
        compiler_params=_params(("arbitrary",), _vmem_limit(4 * _nbytes((tm, LANES), F32))),
        name="rope_tables",
    )(positions.reshape(t, 1), freq)


def _rownorm_kernel(x_ref, g_ref, *rest, modulate):
    o_ref = rest[-1]
    x = x_ref[...]
    y = x * lax.rsqrt(jnp.mean(x * x, axis=-1, keepdims=True) + EPS) * g_ref[...]
    if modulate:
        sc_ref, sh_ref = rest[0], rest[1]
        y = y * (1.0 + sc_ref[...]) + sh_ref[...]
    o_ref[...] = y.astype(o_ref.dtype)


def _rownorm(x, gain, *, col_block=0, width=None, scale=None, shift=None, rows_per_batch=None,
             out_dtype=BF16):
    t = x.shape[0]
    width = x.shape[1] if width is None else width
    tm = min(512, t)
    modulate = scale is not None
    in_specs = [pl.BlockSpec((tm, width), lambda i: (i, col_block)),
                pl.BlockSpec((1, width), lambda i: (0, 0))]
    args = [x, gain.reshape(1, width)]
    if modulate:
        bmap = lambda i: ((i * tm) // rows_per_batch, 0, 0)
        in_specs += [pl.BlockSpec((None, 1, width), bmap), pl.BlockSpec((None, 1, width), bmap)]
        args += [scale, shift]
    blocks = _nbytes((tm, width), F32) + _nbytes((tm, width), out_dtype) + 3 * _nbytes((1, width), F32)
    return pl.pallas_call(
        functools.partial(_rownorm_kernel, modulate=modulate),
        out_shape=jax.ShapeDtypeStruct((t, width), out_dtype),
        grid=(t // tm,),
        in_specs=in_specs,
        out_specs=pl.BlockSpec((tm, width), lambda i: (i, 0)),
        compiler_params=_params(("arbitrary",), _vmem_limit(blocks, 2 * _nbytes((tm, width), F32))),
        name="rownorm",
    )(*args)


def _mm_kernel(x_ref, w_ref, *rest, act, has_bias, n_cast):
    pos = int(has_bias)
    cast_in = rest[pos:pos + n_cast]
    o_ref = rest[pos + n_cast]
    cast_out = rest[pos + n_cast + 1:]
    acc = _dot(x_ref[...], w_ref[...])
    if has_bias:
        acc = acc + rest[0][...]
    if act == "relu2":
        r = jnp.maximum(acc, 0.0)
        acc = r * r
    o_ref[...] = acc.astype(o_ref.dtype)
    for src, dst in zip(cast_in, cast_out):
        dst[...] = src[...].astype(dst.dtype)


def _weight_spec(w, layer, rows, cols, index):
    if w.ndim == 2:
        return pl.BlockSpec((rows, cols), index)
    return pl.BlockSpec((None, rows, cols), lambda *g: (layer,) + tuple(index(*g)))


BF16_ROWS = 16


def _cast_slab(rows, steps):
    slab = BF16_ROWS
    while rows % slab or rows // slab > steps:
        slab += BF16_ROWS
    return slab


def _mm(x, w, *, layer=None, bias=None, act=None, out_dtype=F32, tm=1024, tn=1024, cast=()):
    m, k = x.shape
    n = w.shape[-1]
    tm = min(tm, m)
    tn = min(tn, n)
    gj = n // tn
    steps = (m // tm) * gj
    has_bias = bias is not None
    in_specs = [pl.BlockSpec((tm, k), lambda i, j: (i, 0)),
                _weight_spec(w, layer, k, tn, lambda i, j: (0, j))]
    args = [x, w]
    if has_bias:
        in_specs.append(pl.BlockSpec((1, tn), lambda i, j: (0, j)))
        args.append(bias.reshape(1, n))
    blocks = _nbytes((tm, k), x.dtype) + _nbytes((k, tn), w.dtype) + _nbytes((tm, tn), out_dtype)
    out_shape = [jax.ShapeDtypeStruct((m, n), out_dtype)]
    out_specs = [pl.BlockSpec((tm, tn), lambda i, j: (i, j))]
    for src, src_layer in cast:
        _, rows, cols = src.shape
        slab = _cast_slab(rows, steps)
        n_slabs = rows // slab
        step = lambda i, j, n_slabs=n_slabs: jnp.minimum(i * gj + j, n_slabs - 1)
        in_specs.append(pl.BlockSpec((None, slab, cols),
                                     lambda i, j, src_layer=src_layer, step=step: (src_layer, step(i, j), 0)))
        args.append(src)
        out_shape.append(jax.ShapeDtypeStruct((rows, cols), BF16))
        out_specs.append(pl.BlockSpec((slab, cols), lambda i, j, step=step: (step(i, j), 0)))
        blocks += _nbytes((slab, cols), F32) + _nbytes((slab, cols), BF16)
    outs = pl.pallas_call(
        functools.partial(_mm_kernel, act=act, has_bias=has_bias, n_cast=len(cast)),
        out_shape=out_shape,
        grid=(m // tm, gj),
        in_specs=in_specs,
        out_specs=out_specs,
        compiler_params=_params(("arbitrary", "arbitrary"),
                                _vmem_limit(blocks, 3 * _nbytes((tm, tn), F32))),
        name="mm",
    )(*args)
    return outs if cast else outs[0]


def _mm_res_kernel(*refs, n_pairs, has_bias, nk):
    xs = refs[0:2 * n_pairs:2]
    ws = refs[1:2 * n_pairs:2]
    pos = 2 * n_pairs
    res_ref, gate_ref = refs[pos], refs[pos + 1]
    pos += 2
    bias_ref = refs[pos] if has_bias else None
    pos += int(has_bias)
    o_ref = refs[pos]

    def partial_sum():
        acc = _dot(xs[0][...], ws[0][...])
        for x_ref, w_ref in zip(xs[1:], ws[1:]):
            acc = acc + _dot(x_ref[...], w_ref[...])
        return acc

    def finish(acc):
        if has_bias:
            acc = acc + bias_ref[...]
        o_ref[...] = res_ref[...] + gate_ref[...] * acc

    if nk == 1:
        finish(partial_sum())
        return

    kk = pl.program_id(2)

    @pl.when(kk == 0)
    def _():
        o_ref[...] = partial_sum()

    @pl.when((kk > 0) & (kk < nk - 1))
    def _():
        o_ref[...] += partial_sum()

    @pl.when(kk == nk - 1)
    def _():
        finish(o_ref[...] + partial_sum())


def _mm_res(pairs, res, gate, *, layer=None, bias=None, rows_per_batch, tm=1024, tn=1024, tk=None):
    m, n = res.shape
    k = pairs[0][0].shape[1]
    tk = k if tk is None else tk
    nk = k // tk
    tm = min(tm, rows_per_batch)
    has_bias = bias is not None
    in_specs, args = [], []
    blocks = 0
    for x, w, row0 in pairs:
        in_specs += [pl.BlockSpec((tm, tk), lambda i, j, kk: (i, kk)),
                     _weight_spec(w, layer, tk, tn, lambda i, j, kk, row0=row0: (row0 + kk, j))]
        args += [x, w]
        blocks += _nbytes((tm, tk), x.dtype) + _nbytes((tk, tn), w.dtype)
    in_specs += [pl.BlockSpec((tm, tn), lambda i, j, kk: (i, j)),
                 pl.BlockSpec((None, 1, tn), lambda i, j, kk: ((i * tm) // rows_per_batch, 0, j))]
    args += [res, gate]
    blocks += 2 * _nbytes((tm, tn), F32)
    if has_bias:
        in_specs.append(pl.BlockSpec((1, tn), lambda i, j, kk: (0, j)))
        args.append(bias.reshape(1, n))
    return pl.pallas_call(
        functools.partial(_mm_res_kernel, n_pairs=len(pairs), has_bias=has_bias, nk=nk),
        out_shape=jax.ShapeDtypeStruct((m, n), F32),
        grid=(m // tm, n // tn, nk),
        in_specs=in_specs,
        out_specs=pl.BlockSpec((tm, tn), lambda i, j, kk: (i, j)),
        compiler_params=_params(("arbitrary", "arbitrary", "arbitrary"),
                                _vmem_limit(blocks, 4 * _nbytes((tm, tn), F32))),
        name="mm_res",
    )(*args)


def _attn_kernel(qn_ref, qr_ref, kn_ref, v_ref, kr_ref, cc_ref, ss_ref, o_ref, *, scale, tq):
    seq = qn_ref.shape[0]
    cc = cc_ref[...]
    ss = ss_ref[...]
    lane = lax.broadcasted_iota(jnp.int32, (seq, LANES), 1)
    half = MLA_ROPE // 2

    def rope(x):
        swapped = jnp.where(lane < half, pltpu.roll(x, LANES - half, 1), pltpu.roll(x, half, 1))
        return x * cc + swapped * ss

    qf = jnp.concatenate([(qn_ref[...] * scale).astype(BF16),
                          (rope(qr_ref[...]) * scale).astype(BF16)], axis=-1)
    kf = jnp.concatenate([kn_ref[...].astype(BF16), rope(kr_ref[...]).astype(BF16)], axis=-1)
    vt = v_ref[...].T.astype(BF16)

    k_chunk = lax.broadcasted_iota(jnp.int32, (tq, tq), 0) // MLA_STREAM_CHUNK
    q_chunk = lax.broadcasted_iota(jnp.int32, (tq, tq), 1) // MLA_STREAM_CHUNK
    diag_ok = k_chunk <= q_chunk

    for i in range(seq // tq):
        q0, q1 = i * tq, (i + 1) * tq
        s_d = jnp.where(diag_ok, _dot_nt(kf[q0:q1], qf[q0:q1]), -jnp.inf)
        m = jnp.max(s_d, axis=0, keepdims=True)
        if i > 0:
            s_o = _dot_nt(kf[:q0], qf[q0:q1])
            m = jnp.maximum(m, jnp.max(s_o, axis=0, keepdims=True))
        p_d = jnp.exp2(s_d - m)
        denom = jnp.sum(p_d, axis=0, keepdims=True)
        acc = _dot(vt[:, q0:q1], p_d.astype(BF16))
        if i > 0:
            p_o = jnp.exp2(s_o - m)
            denom = denom + jnp.sum(p_o, axis=0, keepdims=True)
            acc = acc + _dot(vt[:, :q0], p_o.astype(BF16))
        o_ref[q0:q1, :] = (acc / denom).T.astype(o_ref.dtype)


def _attention(q, kv, krdt, cc, ss, *, bsz, seq):
    t = q.shape[0]
    scale = (MLA_NOPE + MLA_ROPE) ** -0.5 * math.log2(math.e)
    blk = lambda off: pl.BlockSpec((seq, LANES), lambda b, h: (b, off + h))
    tok = pl.BlockSpec((seq, LANES), lambda b, h: (b, 0))
    blocks = 7 * _nbytes((seq, LANES), F32) + _nbytes((seq, LANES), BF16)
    return pl.pallas_call(
        functools.partial(_attn_kernel, scale=scale, tq=MLA_Q_TILE),
        out_shape=jax.ShapeDtypeStruct((t, MLA_HEADS * MLA_V_DIM), BF16),
        grid=(bsz, MLA_HEADS),
        in_specs=[blk(0), blk(MLA_HEADS), blk(0), blk(MLA_HEADS), tok, tok, tok],
        out_specs=pl.BlockSpec((seq, MLA_V_DIM), lambda b, h: (b, h)),
        compiler_params=_params(("arbitrary", "arbitrary"), _vmem_limit(blocks, 16 << 20)),
        name="mla_attention",
    )(q, q, kv, kv, krdt, cc, ss)


def _ssd_kernel(z_ref, x_ref, bc_ref, dk_ref, cwx_ref, cbx_ref, cwbc_ref, cbbc_ref,
                dtb_ref, alog_ref, dskip_ref, gain_ref, o_ref,
                tailx_ref, tailbc_ref, state_ref):
    q = x_ref.shape[0]
    p = SSD_HEAD_DIM
    gw = SSD_HEADS_PER_GROUP * p
    n = SSD_STATE

    @pl.when(pl.program_id(1) == 0)
    def _():
        tailx_ref[...] = jnp.zeros_like(tailx_ref)
        tailbc_ref[...] = jnp.zeros_like(tailbc_ref)
        state_ref[...] = jnp.zeros_like(state_ref)

    def conv_silu(tail_ref, blk_ref, w_ref, b_ref):
        blk = blk_ref[...]
        y = _causal_conv(tail_ref[...], blk, w_ref[...], b_ref[...])
        tail_ref[...] = blk[q - SUBLANES:]
        return _silu(y)

    xc = conv_silu(tailx_ref, x_ref, cwx_ref, cbx_ref)
    bcc = conv_silu(tailbc_ref, bc_ref, cwbc_ref, cbbc_ref)

    lane = lax.broadcasted_iota(jnp.int32, (q, LANES), 1)
    lane1 = lax.broadcasted_iota(jnp.int32, (1, LANES), 1)
    is_dt = (lane1 >= DT_LANE0) & (lane1 < DT_LANE0 + SSD_HEADS)
    dt = _softplus(dk_ref[...] + dtb_ref[...])
    a_row = jnp.where(is_dt, -jnp.exp(alog_ref[...]) * LOG2_E, 0.0)
    ad = dt * a_row

    rr = lax.broadcasted_iota(jnp.int32, (q, q), 0)
    cc_ = lax.broadcasted_iota(jnp.int32, (q, q), 1)
    causal = cc_ <= rr
    tri = jnp.where(causal, 1.0, 0.0).astype(BF16)
    a_cs = _select_dot(tri, ad)
    a_cs_t = a_cs.T

    er = lax.broadcasted_iota(jnp.int32, (LANES, SSD_WIDTH), 0)
    ec = lax.broadcasted_iota(jnp.int32, (LANES, SSD_WIDTH), 1)
    expand = jnp.where(er - DT_LANE0 == ec // p, 1.0, 0.0).astype(BF16)
    dt_x = _dot_select(dt, expand)
    acs_x = _dot_select(a_cs, expand)
    dec_x = jnp.exp2(acs_x)
    aend_x = acs_x[q - 1:q, :]
    xs = xc * dt_x
    xs_end = xs * jnp.exp2(aend_x - acs_x)
    chunk_decay = dec_x[q - 1:q, :]

    lo_half = lane < p
    zs = _silu(z_ref[...])
    dskip = dskip_ref[...]
    gain = gain_ref[...]

    for g in range(SSD_GROUPS):
        gs = slice(g * gw, (g + 1) * gw)
        b_g = bcc[:, g * n:(g + 1) * n].astype(BF16)
        c_g = bcc[:, SSD_BC_WIDTH + g * n:SSD_BC_WIDTH + (g + 1) * n].astype(BF16)
        cb = _dot_nt(c_g, b_g)
        s_prev = state_ref[g]
        y_off = _dot(c_g, s_prev.astype(BF16)) * dec_x[:, gs]
        s_new = _dot_tn(b_g, xs_end[:, gs].astype(BF16))
        state_ref[g] = chunk_decay[:, gs] * s_prev + s_new

        pair_out = []
        for pr in range(SSD_HEADS_PER_GROUP // 2):
            ps = slice(g * gw + pr * LANES, g * gw + (pr + 1) * LANES)
            xs_pair = xs[:, ps]
            acc = None
            for hh in range(2):
                h = g * SSD_HEADS_PER_GROUP + pr * 2 + hh
                col = jnp.sum(jnp.where(lane == DT_LANE0 + h, a_cs, 0.0), axis=-1, keepdims=True)
                row = a_cs_t[DT_LANE0 + h:DT_LANE0 + h + 1, :]
                seg = jnp.exp2(jnp.where(causal, col - row, -jnp.inf))
                mh = (cb * seg).astype(BF16)
                keep = lo_half if hh == 0 else jnp.logical_not(lo_half)
                term = _dot(mh, jnp.where(keep, xs_pair, 0.0).astype(BF16))
                acc = term if acc is None else acc + term
            pair_out.append(acc)
        y = jnp.concatenate(pair_out, axis=-1) + y_off + dskip[:, gs] * xc[:, gs]
        y = y * zs[:, gs]
        y = y * lax.rsqrt(jnp.mean(y * y, axis=-1, keepdims=True) + EPS) * gain[:, gs]
        o_ref[:, gs] = y.astype(o_ref.dtype)


def _ssd(main, krdt, conv_w, conv_b, dt_bias, a_log, d_skip, ssd_norm, *, bsz, seq):
    t = main.shape[0]
    q = SSD_TILE
    steps = seq // q
    w = SSD_WIDTH
    row = lambda off: pl.BlockSpec((q, w), lambda b, c: (b * steps + c, off))
    full = lambda r: pl.BlockSpec((r, w), lambda b, c: (0, 0))
    lane_row = pl.BlockSpec((1, LANES), lambda b, c: (0, 0))
    pad = lambda v: jnp.zeros((1, LANES), F32).at[0, DT_LANE0:DT_LANE0 + SSD_HEADS].set(v)
    blocks = 3 * _nbytes((q, w), F32) + _nbytes((q, LANES), F32) + _nbytes((q, w), BF16) \
        + 12 * _nbytes((1, w), F32)
    scratch_bytes = 2 * _nbytes((SUBLANES, w), F32) + _nbytes((SSD_GROUPS, SSD_STATE, w // SSD_GROUPS), F32)
    return pl.pallas_call(
        _ssd_kernel,
        out_shape=jax.ShapeDtypeStruct((t, w), BF16),
        grid=(bsz, steps),
        in_specs=[row(0), row(1), row(2),
                  pl.BlockSpec((q, LANES), lambda b, c: (b * steps + c, 0)),
                  full(SSD_CONV), full(1), full(SSD_CONV), full(1),
                  lane_row, lane_row, full(1), full(1)],
        out_specs=pl.BlockSpec((q, w), lambda b, c: (b * steps + c, 0)),
        scratch_shapes=[pltpu.VMEM((SUBLANES, w), F32), pltpu.VMEM((SUBLANES, w), F32),
                        pltpu.VMEM((SSD_GROUPS, SSD_STATE, w // SSD_GROUPS), F32)],
        compiler_params=_params(("arbitrary", "arbitrary"),
                                _vmem_limit(blocks, scratch_bytes + (24 << 20))),
        name="ssd_mixer",
    )(main, main, main, krdt,
      conv_w[:, :w], conv_b[:w].reshape(1, w), conv_w[:, w:], conv_b[w:].reshape(1, w),
      pad(dt_bias), pad(a_log),
      jnp.repeat(d_skip, SSD_HEAD_DIM).reshape(1, w), ssd_norm.reshape(1, w))


def _rglru_kernel(x_ref, wy_ref, by_ref, wx_ref, bx_ref, cw_ref, cb_ref, wa_ref, ba_ref,
                  wi_ref, bi_ref, lam_ref, o_ref, tail_ref, h_ref, ux_buf, gy_buf, *, tiles_per_batch):
    i = pl.program_id(0)
    j = pl.program_id(1)
    tm, cols = o_ref.shape
    tl = LRU_TILE
    groups = tl // SUBLANES
    nblk = cols // LRU_BLOCK
    w = cw_ref[...]
    cb = cb_ref[...]
    ba = ba_ref[...]
    bi = bi_ref[...]
    by = by_ref[...]
    bx = bx_ref[...]
    neg_c_sp = -LRU_C * LOG2_E * _softplus(-lam_ref[...])
    sub = lax.broadcasted_iota(jnp.int32, (groups, SUBLANES, cols), 1)

    @pl.when(i % tiles_per_batch == 0)
    def _():
        tail_ref[j] = jnp.zeros((SUBLANES, cols), F32)
        h_ref[j] = jnp.zeros((SUBLANES, cols), F32)

    hist = tail_ref[j]
    h_in = h_ref[j]
    def project(s):
        xs = x_ref[s * tl:(s + 1) * tl, :]
        ux_buf[s % LRU_SLOTS] = _dot(xs, wx_ref[...]) + bx
        gy_buf[s % LRU_SLOTS] = _dot(xs, wy_ref[...]) + by

    def conv(s, hist):
        blk = ux_buf[s % LRU_SLOTS]
        return _causal_conv(hist, blk, w, cb), blk[tl - SUBLANES:]

    def gate_logits(u):
        ub = u.astype(BF16)
        ra, ia = [], []
        for bk in range(nblk):
            cs = slice(bk * LRU_BLOCK, (bk + 1) * LRU_BLOCK)
            ra.append(_dot(ub[:, cs], wa_ref[bk]))
            ia.append(_dot(ub[:, cs], wi_ref[bk]))
        return jnp.concatenate(ra, axis=-1) + ba, jnp.concatenate(ia, axis=-1) + bi

    n_sub = tm // tl
    for s in range(min(LRU_SLOTS - 1, n_sub)):
        project(s)
    u, hist = conv(0, hist)
    for s in range(n_sub):
        rows = slice(s * tl, (s + 1) * tl)
        r_logit, i_logit = gate_logits(u)
        if s + LRU_SLOTS - 1 < n_sub:
            project(s + LRU_SLOTS - 1)
        gy = gy_buf[s % LRU_SLOTS]
        r_gate = jax.nn.sigmoid(r_logit)
        i_gate = jax.nn.sigmoid(i_logit)
        a2 = jnp.exp2(neg_c_sp * r_gate)
        b2 = jnp.sqrt(1.0 - a2 * a2) * (i_gate * u)

        a = a2.reshape(groups, SUBLANES, cols)
        b = b2.reshape(groups, SUBLANES, cols)
        for d in (1, 2, 4):
            ok = sub >= d
            a_sh = jnp.where(ok, pltpu.roll(a, d, 1), 1.0)
            b_sh = jnp.where(ok, pltpu.roll(b, d, 1), 0.0)
            b = b + a * b_sh
            a = a * a_sh

        outs = []
        for r in range(groups):
            h_blk = b[r] + a[r] * h_in
            outs.append(h_blk)
            h_in = jnp.broadcast_to(h_blk[SUBLANES - 1:SUBLANES], (SUBLANES, cols))
        hs = jnp.concatenate(outs, axis=0)
        o_ref[rows, :] = (hs * jax.nn.gelu(gy)).astype(o_ref.dtype)
        if s + 1 < n_sub:
            u, hist = conv(s + 1, hist)
    tail_ref[j] = hist
    h_ref[j] = h_in


def _rglru(hm, layer, w_y, b_y, w_x, b_x, conv_w, conv_b, w_a, b_a, w_i, b_i, lam, *, seq):
    t, d = hm.shape
    r = w_x.shape[-1]
    cols = LRU_COLS
    nblk = cols // LRU_BLOCK
    ncb = r // cols
    tm = min(1024, seq)
    vec = lambda: pl.BlockSpec((1, cols), lambda i, j: (0, j))
    proj = pl.BlockSpec((d, cols), lambda i, j: (0, j))
    wblk = pl.BlockSpec((None, nblk, LRU_BLOCK, LRU_BLOCK), lambda i, j: (layer, j, 0, 0))
    blocks = _nbytes((tm, d), BF16) + 2 * _nbytes((d, cols), BF16) + _nbytes((tm, cols), BF16) \
        + 2 * _nbytes((nblk, LRU_BLOCK, LRU_BLOCK), BF16) + 10 * _nbytes((1, cols), F32)
    scratch_bytes = 2 * _nbytes((ncb, SUBLANES, cols), F32) + 2 * LRU_SLOTS * _nbytes((LRU_TILE, cols), F32)
    return pl.pallas_call(
        functools.partial(_rglru_kernel, tiles_per_batch=seq // tm),
        out_shape=jax.ShapeDtypeStruct((t, r), BF16),
        grid=(t // tm, ncb),
        in_specs=[pl.BlockSpec((tm, d), lambda i, j: (i, 0)),
                  proj, vec(), proj, vec(),
                  pl.BlockSpec((LRU_CONV, cols), lambda i, j: (0, j)), vec(),
                  wblk, vec(), wblk, vec(), vec()],
        out_specs=pl.BlockSpec((tm, cols), lambda i, j: (i, j)),
        scratch_shapes=[pltpu.VMEM((ncb, SUBLANES, cols), F32), pltpu.VMEM((ncb, SUBLANES, cols), F32),
                        pltpu.VMEM((LRU_SLOTS, LRU_TILE, cols), F32),
                        pltpu.VMEM((LRU_SLOTS, LRU_TILE, cols), F32)],
        compiler_params=_params(("arbitrary", "arbitrary"),
                                _vmem_limit(blocks, scratch_bytes + (16 << 20))),
        name="rglru_mixer",
    )(hm, w_y, b_y.reshape(1, r), w_x, b_x.reshape(1, r), conv_w, conv_b.reshape(1, r),
      w_a, b_a.reshape(1, r), w_i, b_i.reshape(1, r), lam.reshape(1, r))


def _even_in_weights(w_in):
    d = w_in.shape[0]
    o_z, o_xbc = 0, SSD_WIDTH
    o_dt = o_xbc + SSD_WIDTH + 2 * SSD_BC_WIDTH
    o_cq = o_dt + SSD_HEADS
    q_rank = d // 4
    kv_rank = d // 8
    o_ckv = o_cq + q_rank
    o_kr = o_ckv + kv_rank
    w_main = jnp.concatenate([w_in[:, o_z:o_dt], w_in[:, o_cq:o_kr]], axis=1).astype(BF16)
    w_small = jnp.concatenate(
        [w_in[:, o_kr:o_kr + MLA_ROPE], w_in[:, o_dt:o_cq],
         jnp.zeros((d, LANES - MLA_ROPE - SSD_HEADS), w_in.dtype)], axis=1).astype(BF16)
    return w_main, w_small, q_rank, kv_rank


def _q_weights(w_uq):
    r = w_uq.shape[0]
    w = w_uq.reshape(r, MLA_HEADS, MLA_NOPE + MLA_ROPE)
    nope = w[:, :, :MLA_NOPE].reshape(r, MLA_HEADS * MLA_NOPE)
    rope = jnp.pad(w[:, :, MLA_NOPE:], ((0, 0), (0, 0), (0, LANES - MLA_ROPE)))
    return jnp.concatenate([nope, rope.reshape(r, MLA_HEADS * LANES)], axis=1).astype(BF16)


def _kv_weights(w_ukv):
    r = w_ukv.shape[0]
    w = w_ukv.reshape(r, MLA_HEADS, MLA_NOPE + MLA_V_DIM)
    k = w[:, :, :MLA_NOPE].reshape(r, MLA_HEADS * MLA_NOPE)
    v = w[:, :, MLA_NOPE:].reshape(r, MLA_HEADS * MLA_V_DIM)
    return jnp.concatenate([k, v], axis=1).astype(BF16)


def kernel(x, c, positions, ada_w, ada_b, ada_table, norm_mix, norm_ffn, norm_final, ffn_up, ffn_down, e_w_in, e_conv_w, e_conv_b, e_dt_bias, e_a_log, e_d_skip, e_ssd_norm, e_q_norm, e_w_uq, e_kv_norm, e_w_ukv, e_w_out, o_w_y, o_b_y, o_w_x, o_b_x, o_conv_w, o_conv_b, o_w_a, o_b_a, o_w_i, o_b_i, o_lam, o_w_out, o_b_out):
    bsz, seq, d = x.shape
    t = bsz * seq
    depth = ada_table.shape[0]
    xf = x.reshape(t, d)

    cc, ss = _rope_tables(positions)
    mod_all = _ada_proj(c, ada_w, ada_b).reshape(bsz, N_MOD, d)

    o_w_a_b, o_w_i_b = o_w_a.astype(BF16), o_w_i.astype(BF16)

    def mixer_casts(layer):
        jj = layer // 2
        if layer % 2 == 0:
            return {"e_out": (e_w_out, jj)}
        return {"o_y": (o_w_y, jj), "o_x": (o_w_x, jj), "o_out": (o_w_out, jj)}

    def mm_casting(xin, w, wanted, **kw):
        names = list(wanted)
        outs = _mm(xin, w, cast=[wanted[n] for n in names], **kw)
        return outs[0], dict(zip(names, outs[1:]))

    wb = {}

    for layer in range(depth):
        mod = mod_all + ada_table[layer]
        shift_m, scale_m, gate_m, shift_f, scale_f, gate_f = (mod[:, i:i + 1] for i in range(N_MOD))
        hm = _rownorm(xf, norm_mix[layer], scale=scale_m, shift=shift_m, rows_per_batch=seq)
        j = layer // 2
        if layer % 2 == 0:
            w_main, w_small, q_rank, kv_rank = _even_in_weights(e_w_in[j])
            if layer == 0:
                main, first = mm_casting(hm, w_main, {"up": (ffn_up, 0), **mixer_casts(0)}, tn=768)
                wb.update(first)
            else:
                main = _mm(hm, w_main, tn=768)
            krdt = _mm(hm, w_small, tn=LANES)
            o_cq = 2 * SSD_WIDTH + 2 * SSD_BC_WIDTH
            cqn = _rownorm(main, e_q_norm[j], col_block=o_cq // q_rank, width=q_rank)
            ckvn = _rownorm(main, e_kv_norm[j], col_block=(o_cq + q_rank) // kv_rank, width=kv_rank)
            q = _mm(cqn, _q_weights(e_w_uq[j]))
            kv = _mm(ckvn, _kv_weights(e_w_ukv[j]))
            o_attn = _attention(q, kv, krdt, cc, ss, bsz=bsz, seq=seq)
            y_ssd = _ssd(main, krdt, e_conv_w[j], e_conv_b[j], e_dt_bias[j], e_a_log[j],
                         e_d_skip[j], e_ssd_norm[j], bsz=bsz, seq=seq)
            xf = _mm_res([(y_ssd, wb["e_out"], 0), (o_attn, wb["e_out"], 1)], xf, gate_m,
                         rows_per_batch=seq)
        else:
            y_lru = _rglru(hm, j, wb["o_y"], o_b_y[j], wb["o_x"], o_b_x[j], o_conv_w[j], o_conv_b[j],
                           o_w_a_b, o_b_a[j], o_w_i_b, o_b_i[j], o_lam[j], seq=seq)
            r = y_lru.shape[1]
            xf = _mm_res([(y_lru, wb["o_out"], 0)], xf, gate_m, bias=o_b_out[j],
                         rows_per_batch=seq, tk=r // 2)
        hf = _rownorm(xf, norm_ffn[layer], scale=scale_f, shift=shift_f, rows_per_batch=seq)
        wanted = {"down": (ffn_down, layer)}
        if layer + 1 < depth:
            wanted.update({"up": (ffn_up, layer + 1), **mixer_casts(layer + 1)})
        hid, nxt = mm_casting(hf, wb["up"], wanted, act="relu2", out_dtype=BF16)
        wb.update(nxt)
        xf = _mm_res([(hid, wb["down"], 0)], xf, gate_f, rows_per_batch=seq, tk=4096)

    out = _rownorm(xf, norm_final, out_dtype=F32)
    return out.reshape(bsz, seq, d)
```

```python
import functools
import math

import jax
import jax.numpy as jnp
from jax import lax
from jax.experimental import pallas as pl
from jax.experimental.pallas import tpu as pltpu

F32 = jnp.float32
BF16 = jnp.bfloat16

EPS = 1e-6
LOG2_E = math.log2(math.e)
ROPE_THETA = 10000.0
N_MOD = 6

SSD_HEAD_DIM = 64
SSD_GROUPS = 8
SSD_HEADS_PER_GROUP = 4
SSD_STATE = 128
SSD_CONV = 4
SSD_HEADS = SSD_GROUPS * SSD_HEADS_PER_GROUP
SSD_WIDTH = SSD_HEADS * SSD_HEAD_DIM
SSD_BC_WIDTH = SSD_GROUPS * SSD_STATE
SSD_TILE = 256

MLA_HEADS = 16
MLA_NOPE = 128
MLA_ROPE = 64
MLA_V_DIM = 128
MLA_STREAM_CHUNK = 64
MLA_Q_TILE = 256

LRU_BLOCK = 256
LRU_CONV = 4
LRU_C = 8.0
LRU_TILE = 256
LRU_COLS = 512
LRU_SLOTS = 3

LANES = 128
SUBLANES = 8
DT_LANE0 = 64
VMEM_LIMIT_CAP = 60000 * 1024


def _vmem_limit(block_bytes, extra=0):
    need = 2 * block_bytes + extra + (4 << 20)
    return int(min(max(need, 16 << 20), VMEM_LIMIT_CAP))


def _params(semantics, vmem, flags=None):
    return pltpu.CompilerParams(dimension_semantics=semantics, vmem_limit_bytes=vmem, flags=flags)


def _nbytes(shape, dtype):
    n = 1
    for s in shape:
        n *= s
    return n * jnp.dtype(dtype).itemsize


def _silu(x):
    return x * jax.nn.sigmoid(x)


def _softplus(x):
    return jnp.maximum(x, 0.0) + jnp.log(1.0 + jnp.exp(-jnp.abs(x)))


def _split3(v):
    a1 = v.astype(BF16)
    r1 = v - a1.astype(F32)
    a2 = r1.astype(BF16)
    a3 = (r1 - a2.astype(F32)).astype(BF16)
    return a1, a2, a3


def _causal_conv(hist, blk, w, bias):
    rows, cols = blk.shape
    taps = w.shape[0]
    groups = rows // SUBLANES
    x3 = jnp.concatenate([hist, blk], axis=0).reshape(groups + 1, SUBLANES, cols)
    sub = lax.broadcasted_iota(jnp.int32, (groups, SUBLANES, cols), 1)
    y = bias + w[taps - 1:taps] * x3[1:]
    for s in range(1, taps):
        r = pltpu.roll(x3, s, 1)
        y = y + w[taps - 1 - s:taps - s] * jnp.where(sub >= s, r[1:], r[:-1])
    return y.reshape(rows, cols)


def _dot(a, b):
    return jnp.dot(a, b, preferred_element_type=F32)


def _dot_nt(a, b):
    return lax.dot_general(a, b, (((1,), (1,)), ((), ())), preferred_element_type=F32)


def _dot_tn(a, b):
    return lax.dot_general(a, b, (((0,), (0,)), ((), ())), preferred_element_type=F32)


def _select_dot(sel, v):
    a1, a2, a3 = _split3(v)
    return _dot(sel, a1) + _dot(sel, a2) + _dot(sel, a3)


def _dot_select(v, sel):
    a1, a2, a3 = _split3(v)
    return _dot(a1, sel) + _dot(a2, sel) + _dot(a3, sel)


def _ada_kernel(c_ref, w_ref, b_ref, o_ref):
    s = _silu(c_ref[...]).astype(BF16)
    o_ref[...] = _dot(s, w_ref[...].astype(BF16)) + b_ref[...]


def _ada_proj(c, w, b):
    bsz, d = c.shape
    n = w.shape[1]
    tn = 512
    blocks = _nbytes((d, tn), F32) + _nbytes((bsz, d), F32) + 2 * _nbytes((bsz, tn), F32)
    return pl.pallas_call(
        _ada_kernel,
        out_shape=jax.ShapeDtypeStruct((bsz, n), F32),
        grid=(n // tn,),
        in_specs=[pl.BlockSpec((bsz, d), lambda j: (0, 0)),
                  pl.BlockSpec((d, tn), lambda j: (0, j)),
                  pl.BlockSpec((1, tn), lambda j: (0, j))],
        out_specs=pl.BlockSpec((bsz, tn), lambda j: (0, j)),
        compiler_params=_params(("arbitrary",), _vmem_limit(blocks, _nbytes((d, tn), BF16))),
        name="ada_proj",
    )(c, w, b.reshape(1, n))


def _rope_table_kernel(pos_ref, freq_ref, cc_ref, ss_ref):
    ang = pos_ref[...].astype(F32) * freq_ref[...]
    lane = lax.broadcasted_iota(jnp.int32, ang.shape, 1)
    half = MLA_ROPE // 2
    c = jnp.cos(ang)
    s = jnp.sin(ang)
    cc_ref[...] = jnp.where(lane < MLA_ROPE, c, 0.0)
    ss_ref[...] = jnp.where(lane < half, -s, jnp.where(lane < MLA_ROPE, s, 0.0))


def _rope_tables(positions):
    t = positions.size
    half = MLA_ROPE // 2
    inv_freq = ROPE_THETA ** (-jnp.arange(0, MLA_ROPE, 2, dtype=F32) / MLA_ROPE)
    freq = jnp.concatenate([inv_freq, inv_freq, jnp.zeros((LANES - 2 * half,), F32)]).reshape(1, LANES)
    tm = min(t, 2048)
    return pl.pallas_call(
        _rope_table_kernel,
        out_shape=(jax.ShapeDtypeStruct((t, LANES), F32), jax.ShapeDtypeStruct((t, LANES), F32)),
        grid=(t // tm,),
        in_specs=[pl.BlockSpec((tm, 1), lambda i: (i, 0)),
                  pl.BlockSpec((1, LANES), lambda i: (0, 0))],
        out_specs=(pl.BlockSpec((tm, LANES), lambda i: (i, 0)),
                   pl.BlockSpec((tm, LANES), lambda i: (i, 0))),
        compiler_params=_params(("arbitrary",), _vmem_limit(4 * _nbytes((tm, LANES), F32))),
        name="rope_tables",
    )(positions.reshape(t, 1), freq)


def _rownorm_kernel(x_ref, g_ref, *rest, modulate):
    o_ref = rest[-1]
    x = x_ref[...]
    y = x * lax.rsqrt(jnp.mean(x * x, axis=-1, keepdims=True) + EPS) * g_ref[...]
    if modulate:
        sc_ref, sh_ref = rest[0], rest[1]
        y = y * (1.0 + sc_ref[...]) + sh_ref[...]
    o_ref[...] = y.astype(o_ref.dtype)


def _rownorm(x, gain, *, col_block=0, width=None, scale=None, shift=None, rows_per_batch=None,
             out_dtype=BF16):
    t = x.shape[0]
    width = x.shape[1] if width is None else width
    tm = min(512, t if rows_per_batch is None else rows_per_batch)
    modulate = scale is not None
    in_specs = [pl.BlockSpec((tm, width), lambda i: (i, col_block)),
                pl.BlockSpec((1, width), lambda i: (0, 0))]
    args = [x, gain.reshape(1, width)]
    if modulate:
        bmap = lambda i: ((i * tm) // rows_per_batch, 0, 0)
        in_specs += [pl.BlockSpec((None, 1, width), bmap), pl.BlockSpec((None, 1, width), bmap)]
        args += [scale, shift]
    blocks = _nbytes((tm, width), F32) + _nbytes((tm, width), out_dtype) + 3 * _nbytes((1, width), F32)
    return pl.pallas_call(
        functools.partial(_rownorm_kernel, modulate=modulate),
        out_shape=jax.ShapeDtypeStruct((t, width), out_dtype),
        grid=(t // tm,),
        in_specs=in_specs,
        out_specs=pl.BlockSpec((tm, width), lambda i: (i, 0)),
        compiler_params=_params(("arbitrary",), _vmem_limit(blocks, 2 * _nbytes((tm, width), F32))),
        name="rownorm",
    )(*args)


def _mm_kernel(x_ref, w_ref, *rest, act, has_bias, has_aux, n_cast):
    pos = int(has_bias)
    aux_w_ref = rest[pos] if has_aux else None
    pos += int(has_aux)
    cast_in = rest[pos:pos + n_cast]
    o_ref = rest[pos + n_cast]
    aux_o_ref = rest[pos + n_cast + 1] if has_aux else None
    cast_out = rest[pos + n_cast + 1 + int(has_aux):]
    if has_aux:
        @pl.when(pl.program_id(1) == 0)
        def _():
            aux_o_ref[...] = _dot(x_ref[...], aux_w_ref[...])
    acc = _dot(x_ref[...], w_ref[...])
    if has_bias:
        acc = acc + rest[0][...]
    if act == "relu2":
        r = jnp.maximum(acc, 0.0)
        acc = r * r
    o_ref[...] = acc.astype(o_ref.dtype)
    for src, dst in zip(cast_in, cast_out):
        dst[...] = src[...].astype(dst.dtype)


def _weight_spec(w, layer, rows, cols, index):
    if w.ndim == 2:
        return pl.BlockSpec((rows, cols), index)
    return pl.BlockSpec((None, rows, cols), lambda *g: (layer,) + tuple(index(*g)))


BF16_ROWS = 16


def _cast_slab(rows, steps):
    slab = BF16_ROWS
    while rows % slab or rows // slab > steps:
        slab += BF16_ROWS
    return slab


def _mm(x, w, *, layer=None, bias=None, act=None, out_dtype=F32, tm=1024, tn=1024, cast=(),
        aux_w=None):
    m, k = x.shape
    n = w.shape[-1]
    tm = min(tm, m)
    tn = min(tn, n)
    gj = n // tn
    steps = (m // tm) * gj
    has_bias = bias is not None
    has_aux = aux_w is not None
    in_specs = [pl.BlockSpec((tm, k), lambda i, j: (i, 0)),
                _weight_spec(w, layer, k, tn, lambda i, j: (0, j))]
    args = [x, w]
    if has_bias:
        in_specs.append(pl.BlockSpec((1, tn), lambda i, j: (0, j)))
        args.append(bias.reshape(1, n))
    blocks = _nbytes((tm, k), x.dtype) + _nbytes((k, tn), w.dtype) + _nbytes((tm, tn), out_dtype)
    out_shape = [jax.ShapeDtypeStruct((m, n), out_dtype)]
    out_specs = [pl.BlockSpec((tm, tn), lambda i, j: (i, j))]
    if has_aux:
        na = aux_w.shape[1]
        in_specs.append(pl.BlockSpec((k, na), lambda i, j: (0, 0)))
        args.append(aux_w)
        out_shape.append(jax.ShapeDtypeStruct((m, na), F32))
        out_specs.append(pl.BlockSpec((tm, na), lambda i, j: (i, 0)))
        blocks += _nbytes((k, na), aux_w.dtype) + _nbytes((tm, na), F32)
    for src, src_layer in cast:
        _, rows, cols = src.shape
        slab = _cast_slab(rows, steps)
        n_slabs = rows // slab
        step = lambda i, j, n_slabs=n_slabs: jnp.minimum(i * gj + j, n_slabs - 1)
        in_specs.append(pl.BlockSpec((None, slab, cols),
                                     lambda i, j, src_layer=src_layer, step=step: (src_layer, step(i, j), 0)))
        args.append(src)
        out_shape.append(jax.ShapeDtypeStruct((rows, cols), BF16))
        out_specs.append(pl.BlockSpec((slab, cols), lambda i, j, step=step: (step(i, j), 0)))
        blocks += _nbytes((slab, cols), F32) + _nbytes((slab, cols), BF16)
    outs = pl.pallas_call(
        functools.partial(_mm_kernel, act=act, has_bias=has_bias, has_aux=has_aux, n_cast=len(cast)),
        out_shape=out_shape,
        grid=(m // tm, gj),
        in_specs=in_specs,
        out_specs=out_specs,
        compiler_params=_params(("arbitrary", "arbitrary"),
                                _vmem_limit(blocks, 3 * _nbytes((tm, tn), F32))),
        name="mm",
    )(*args)
    return outs if (cast or has_aux) else outs[0]


def _norm_mm_kernel(x_ref, g_ref, w_ref, o_ref, xn_ref):
    @pl.when(pl.program_id(1) == 0)
    def _():
        x = x_ref[...]
        y = x * lax.rsqrt(jnp.mean(x * x, axis=-1, keepdims=True) + EPS) * g_ref[...]
        xn_ref[...] = y.astype(xn_ref.dtype)

    o_ref[...] = _dot(xn_ref[...], w_ref[...])


def _norm_mm(x, gain, w, *, col_block, tm=1024, tn=1024):
    m = x.shape[0]
    k, n = w.shape
    tm = min(tm, m)
    blocks = _nbytes((tm, k), F32) + _nbytes((k, tn), w.dtype) + _nbytes((tm, tn), F32)
    return pl.pallas_call(
        _norm_mm_kernel,
        out_shape=jax.ShapeDtypeStruct((m, n), F32),
        grid=(m // tm, n // tn),
        in_specs=[pl.BlockSpec((tm, k), lambda i, j: (i, col_block)),
                  pl.BlockSpec((1, k), lambda i, j: (0, 0)),
                  pl.BlockSpec((k, tn), lambda i, j: (0, j))],
        out_specs=pl.BlockSpec((tm, tn), lambda i, j: (i, j)),
        scratch_shapes=[pltpu.VMEM((tm, k), BF16)],
        compiler_params=_params(("arbitrary", "arbitrary"),
                                _vmem_limit(blocks, _nbytes((tm, k), BF16) + 3 * _nbytes((tm, tn), F32))),
        name="norm_mm",
    )(x, gain.reshape(1, k), w)


def _mm_res_kernel(*refs, n_pairs, has_bias, nk):
    xs = refs[0:2 * n_pairs:2]
    ws = refs[1:2 * n_pairs:2]
    pos = 2 * n_pairs
    res_ref, gate_ref = refs[pos], refs[pos + 1]
    pos += 2
    bias_ref = refs[pos] if has_bias else None
    pos += int(has_bias)
    o_ref = refs[pos]

    def partial_sum():
        acc = _dot(xs[0][...], ws[0][...])
        for x_ref, w_ref in zip(xs[1:], ws[1:]):
            acc = acc + _dot(x_ref[...], w_ref[...])
        return acc

    def finish(acc):
        if has_bias:
            acc = acc + bias_ref[...]
        o_ref[...] = res_ref[...] + gate_ref[...] * acc

    if nk == 1:
        finish(partial_sum())
        return

    kk = pl.program_id(2)

    @pl.when(kk == 0)
    def _():
        o_ref[...] = partial_sum()

    @pl.when((kk > 0) & (kk < nk - 1))
    def _():
        o_ref[...] += partial_sum()

    @pl.when(kk == nk - 1)
    def _():
        finish(o_ref[...] + partial_sum())


def _mm_res(pairs, res, gate, *, layer=None, bias=None, rows_per_batch, tm=1024, tn=1024, tk=None):
    m, n = res.shape
    k = pairs[0][0].shape[1]
    tk = k if tk is None else tk
    nk = k // tk
    tm = min(tm, rows_per_batch)
    has_bias = bias is not None
    in_specs, args = [], []
    blocks = 0
    for x, w, row0 in pairs:
        in_specs += [pl.BlockSpec((tm, tk), lambda i, j, kk: (i, kk)),
                     _weight_spec(w, layer, tk, tn, lambda i, j, kk, row0=row0: (row0 + kk, j))]
        args += [x, w]
        blocks += _nbytes((tm, tk), x.dtype) + _nbytes((tk, tn), w.dtype)
    in_specs += [pl.BlockSpec((tm, tn), lambda i, j, kk: (i, j)),
                 pl.BlockSpec((None, 1, tn), lambda i, j, kk: ((i * tm) // rows_per_batch, 0, j))]
    args += [res, gate]
    blocks += 2 * _nbytes((tm, tn), F32)
    if has_bias:
        in_specs.append(pl.BlockSpec((1, tn), lambda i, j, kk: (0, j)))
        args.append(bias.reshape(1, n))
    return pl.pallas_call(
        functools.partial(_mm_res_kernel, n_pairs=len(pairs), has_bias=has_bias, nk=nk),
        out_shape=jax.ShapeDtypeStruct((m, n), F32),
        grid=(m // tm, n // tn, nk),
        in_specs=in_specs,
        out_specs=pl.BlockSpec((tm, tn), lambda i, j, kk: (i, j)),
        compiler_params=_params(("arbitrary", "arbitrary", "arbitrary"),
                                _vmem_limit(blocks, 4 * _nbytes((tm, tn), F32))),
        name="mm_res",
    )(*args)


def _attn_kernel(qn_ref, qr_ref, kn_ref, v_ref, kr_ref, cc_ref, ss_ref, o_ref, *, scale, tq):
    seq = qn_ref.shape[0]
    cc = cc_ref[...]
    ss = ss_ref[...]
    lane = lax.broadcasted_iota(jnp.int32, (seq, LANES), 1)
    half = MLA_ROPE // 2

    def rope(x):
        swapped = jnp.where(lane < half, pltpu.roll(x, LANES - half, 1), pltpu.roll(x, half, 1))
        return x * cc + swapped * ss

    qf = jnp.concatenate([(qn_ref[...] * scale).astype(BF16),
                          (rope(qr_ref[...]) * scale).astype(BF16)], axis=-1)
    kf = jnp.concatenate([kn_ref[...].astype(BF16), rope(kr_ref[...]).astype(BF16)], axis=-1)
    vt = v_ref[...].T.astype(BF16)

    k_chunk = lax.broadcasted_iota(jnp.int32, (tq, tq), 0) // MLA_STREAM_CHUNK
    q_chunk = lax.broadcasted_iota(jnp.int32, (tq, tq), 1) // MLA_STREAM_CHUNK
    diag_ok = k_chunk <= q_chunk

    for i in range(seq // tq):
        q0, q1 = i * tq, (i + 1) * tq
        s_d = jnp.where(diag_ok, _dot_nt(kf[q0:q1], qf[q0:q1]), -jnp.inf)
        m = jnp.max(s_d, axis=0, keepdims=True)
        if i > 0:
            s_o = _dot_nt(kf[:q0], qf[q0:q1])
            m = jnp.maximum(m, jnp.max(s_o, axis=0, keepdims=True))
        p_d = jnp.exp2(s_d - m)
        denom = jnp.sum(p_d, axis=0, keepdims=True)
        acc = _dot(vt[:, q0:q1], p_d.astype(BF16))
        if i > 0:
            p_o = jnp.exp2(s_o - m)
            denom = denom + jnp.sum(p_o, axis=0, keepdims=True)
            acc = acc + _dot(vt[:, :q0], p_o.astype(BF16))
        o_ref[q0:q1, :] = (acc / denom).T.astype(o_ref.dtype)


def _attention(q, kv, krdt, cc, ss, *, bsz, seq):
    t = q.shape[0]
    scale = (MLA_NOPE + MLA_ROPE) ** -0.5 * math.log2(math.e)
    blk = lambda off: pl.BlockSpec((seq, LANES), lambda b, h: (b, off + h))
    tok = pl.BlockSpec((seq, LANES), lambda b, h: (b, 0))
    blocks = 7 * _nbytes((seq, LANES), F32) + _nbytes((seq, LANES), BF16)
    return pl.pallas_call(
        functools.partial(_attn_kernel, scale=scale, tq=MLA_Q_TILE),
        out_shape=jax.ShapeDtypeStruct((t, MLA_HEADS * MLA_V_DIM), BF16),
        grid=(bsz, MLA_HEADS),
        in_specs=[blk(0), blk(MLA_HEADS), blk(0), blk(MLA_HEADS), tok, tok, tok],
        out_specs=pl.BlockSpec((seq, MLA_V_DIM), lambda b, h: (b, h)),
        compiler_params=_params(("arbitrary", "arbitrary"), _vmem_limit(blocks, 16 << 20)),
        name="mla_attention",
    )(q, q, kv, kv, krdt, cc, ss)


def _ssd_kernel(z_ref, x_ref, bc_ref, dk_ref, cwx_ref, cbx_ref, cwbc_ref, cbbc_ref,
                dtb_ref, alog_ref, dskip_ref, gain_ref, o_ref,
                tailx_ref, tailbc_ref, state_ref):
    q = x_ref.shape[0]
    p = SSD_HEAD_DIM
    gw = SSD_HEADS_PER_GROUP * p
    n = SSD_STATE

    @pl.when(pl.program_id(1) == 0)
    def _():
        tailx_ref[...] = jnp.zeros_like(tailx_ref)
        tailbc_ref[...] = jnp.zeros_like(tailbc_ref)
        state_ref[...] = jnp.zeros_like(state_ref)

    def conv_silu(tail_ref, blk_ref, w_ref, b_ref):
        blk = blk_ref[...]
        y = _causal_conv(tail_ref[...], blk, w_ref[...], b_ref[...])
        tail_ref[...] = blk[q - SUBLANES:]
        return _silu(y)

    xc = conv_silu(tailx_ref, x_ref, cwx_ref, cbx_ref)
    bcc = conv_silu(tailbc_ref, bc_ref, cwbc_ref, cbbc_ref)

    lane = lax.broadcasted_iota(jnp.int32, (q, LANES), 1)
    lane1 = lax.broadcasted_iota(jnp.int32, (1, LANES), 1)
    is_dt = (lane1 >= DT_LANE0) & (lane1 < DT_LANE0 + SSD_HEADS)
    dt = _softplus(dk_ref[...] + dtb_ref[...])
    a_row = jnp.where(is_dt, -jnp.exp(alog_ref[...]) * LOG2_E, 0.0)
    ad = dt * a_row

    rr = lax.broadcasted_iota(jnp.int32, (q, q), 0)
    cc_ = lax.broadcasted_iota(jnp.int32, (q, q), 1)
    causal = cc_ <= rr
    tri = jnp.where(causal, 1.0, 0.0).astype(BF16)
    a_cs = _select_dot(tri, ad)
    a_cs_t = a_cs.T

    er = lax.broadcasted_iota(jnp.int32, (LANES, SSD_WIDTH), 0)
    ec = lax.broadcasted_iota(jnp.int32, (LANES, SSD_WIDTH), 1)
    expand = jnp.where(er - DT_LANE0 == ec // p, 1.0, 0.0).astype(BF16)
    dt_x = _dot_select(dt, expand)
    acs_x = _dot_select(a_cs, expand)
    dec_x = jnp.exp2(acs_x)
    aend_x = acs_x[q - 1:q, :]
    xs = xc * dt_x
    xs_end = xs * jnp.exp2(aend_x - acs_x)
    chunk_decay = dec_x[q - 1:q, :]

    lo_half = lane < p
    zs = _silu(z_ref[...])
    dskip = dskip_ref[...]
    gain = gain_ref[...]

    for g in range(SSD_GROUPS):
        gs = slice(g * gw, (g + 1) * gw)
        b_g = bcc[:, g * n:(g + 1) * n].astype(BF16)
        c_g = bcc[:, SSD_BC_WIDTH + g * n:SSD_BC_WIDTH + (g + 1) * n].astype(BF16)
        cb = _dot_nt(c_g, b_g)
        s_prev = state_ref[g]
        y_off = _dot(c_g, s_prev.astype(BF16)) * dec_x[:, gs]
        s_new = _dot_tn(b_g, xs_end[:, gs].astype(BF16))
        state_ref[g] = chunk_decay[:, gs] * s_prev + s_new

        pair_out = []
        for pr in range(SSD_HEADS_PER_GROUP // 2):
            ps = slice(g * gw + pr * LANES, g * gw + (pr + 1) * LANES)
            xs_pair = xs[:, ps]
            acc = None
            for hh in range(2):
                h = g * SSD_HEADS_PER_GROUP + pr * 2 + hh
                col = jnp.sum(jnp.where(lane == DT_LANE0 + h, a_cs, 0.0), axis=-1, keepdims=True)
                row = a_cs_t[DT_LANE0 + h:DT_LANE0 + h + 1, :]
                seg = jnp.exp2(jnp.where(causal, col - row, -jnp.inf))
                mh = (cb * seg).astype(BF16)
                keep = lo_half if hh == 0 else jnp.logical_not(lo_half)
                term = _dot(mh, jnp.where(keep, xs_pair, 0.0).astype(BF16))
                acc = term if acc is None else acc + term
            pair_out.append(acc)
        y = jnp.concatenate(pair_out, axis=-1) + y_off + dskip[:, gs] * xc[:, gs]
        y = y * zs[:, gs]
        y = y * lax.rsqrt(jnp.mean(y * y, axis=-1, keepdims=True) + EPS) * gain[:, gs]
        o_ref[:, gs] = y.astype(o_ref.dtype)


def _ssd(main, krdt, conv_w, conv_b, dt_bias, a_log, d_skip, ssd_norm, *, bsz, seq):
    t = main.shape[0]
    q = SSD_TILE
    steps = seq // q
    w = SSD_WIDTH
    row = lambda off: pl.BlockSpec((q, w), lambda b, c: (b * steps + c, off))
    full = lambda r: pl.BlockSpec((r, w), lambda b, c: (0, 0))
    lane_row = pl.BlockSpec((1, LANES), lambda b, c: (0, 0))
    pad = lambda v: jnp.zeros((1, LANES), F32).at[0, DT_LANE0:DT_LANE0 + SSD_HEADS].set(v)
    blocks = 3 * _nbytes((q, w), F32) + _nbytes((q, LANES), F32) + _nbytes((q, w), BF16) \
        + 12 * _nbytes((1, w), F32)
    scratch_bytes = 2 * _nbytes((SUBLANES, w), F32) + _nbytes((SSD_GROUPS, SSD_STATE, w // SSD_GROUPS), F32)
    return pl.pallas_call(
        _ssd_kernel,
        out_shape=jax.ShapeDtypeStruct((t, w), BF16),
        grid=(bsz, steps),
        in_specs=[row(0), row(1), row(2),
                  pl.BlockSpec((q, LANES), lambda b, c: (b * steps + c, 0)),
                  full(SSD_CONV), full(1), full(SSD_CONV), full(1),
                  lane_row, lane_row, full(1), full(1)],
        out_specs=pl.BlockSpec((q, w), lambda b, c: (b * steps + c, 0)),
        scratch_shapes=[pltpu.VMEM((SUBLANES, w), F32), pltpu.VMEM((SUBLANES, w), F32),
                        pltpu.VMEM((SSD_GROUPS, SSD_STATE, w // SSD_GROUPS), F32)],
        compiler_params=_params(("arbitrary", "arbitrary"),
                                _vmem_limit(blocks, scratch_bytes + (24 << 20))),
        name="ssd_mixer",
    )(main, main, main, krdt,
      conv_w[:, :w], conv_b[:w].reshape(1, w), conv_w[:, w:], conv_b[w:].reshape(1, w),
      pad(dt_bias), pad(a_log),
      jnp.repeat(d_skip, SSD_HEAD_DIM).reshape(1, w), ssd_norm.reshape(1, w))


def _rglru_kernel(x_ref, wy_ref, by_ref, wx_ref, bx_ref, cw_ref, cb_ref, wa_ref, ba_ref,
                  wi_ref, bi_ref, lam_ref, o_ref, tail_ref, h_ref, ux_buf, gy_buf, *, tiles_per_batch):
    i = pl.program_id(0)
    j = pl.program_id(1)
    tm, cols = o_ref.shape
    tl = LRU_TILE
    groups = tl // SUBLANES
    nblk = cols // LRU_BLOCK
    w = cw_ref[...]
    cb = cb_ref[...]
    ba = ba_ref[...]
    bi = bi_ref[...]
    by = by_ref[...]
    bx = bx_ref[...]
    neg_c_sp = -LRU_C * LOG2_E * _softplus(-lam_ref[...])
    sub = lax.broadcasted_iota(jnp.int32, (groups, SUBLANES, cols), 1)

    @pl.when(i % tiles_per_batch == 0)
    def _():
        tail_ref[j] = jnp.zeros((SUBLANES, cols), F32)
        h_ref[j] = jnp.zeros((SUBLANES, cols), F32)

    hist = tail_ref[j]
    h_in = h_ref[j]
    def project(s):
        xs = x_ref[s * tl:(s + 1) * tl, :]
        ux_buf[s % LRU_SLOTS] = _dot(xs, wx_ref[...]) + bx
        gy_buf[s % LRU_SLOTS] = _dot(xs, wy_ref[...]) + by

    def conv(s, hist):
        blk = ux_buf[s % LRU_SLOTS]
        return _causal_conv(hist, blk, w, cb), blk[tl - SUBLANES:]

    def gate_logits(u):
        ub = u.astype(BF16)
        ra, ia = [], []
        for bk in range(nblk):
            cs = slice(bk * LRU_BLOCK, (bk + 1) * LRU_BLOCK)
            ra.append(_dot(ub[:, cs], wa_ref[bk]))
            ia.append(_dot(ub[:, cs], wi_ref[bk]))
        return jnp.concatenate(ra, axis=-1) + ba, jnp.concatenate(ia, axis=-1) + bi

    n_sub = tm // tl
    for s in range(min(LRU_SLOTS - 1, n_sub)):
        project(s)
    u, hist = conv(0, hist)
    for s in range(n_sub):
        rows = slice(s * tl, (s + 1) * tl)
        r_logit, i_logit = gate_logits(u)
        if s + LRU_SLOTS - 1 < n_sub:
            project(s + LRU_SLOTS - 1)
        gy = gy_buf[s % LRU_SLOTS]
        r_gate = jax.nn.sigmoid(r_logit)
        i_gate = jax.nn.sigmoid(i_logit)
        a2 = jnp.exp2(neg_c_sp * r_gate)
        b2 = jnp.sqrt(1.0 - a2 * a2) * (i_gate * u)

        a = a2.reshape(groups, SUBLANES, cols)
        b = b2.reshape(groups, SUBLANES, cols)
        for d in (1, 2, 4):
            ok = sub >= d
            a_sh = jnp.where(ok, pltpu.roll(a, d, 1), 1.0)
            b_sh = jnp.where(ok, pltpu.roll(b, d, 1), 0.0)
            b = b + a * b_sh
            a = a * a_sh

        outs = []
        for r in range(groups):
            h_blk = b[r] + a[r] * h_in
            outs.append(h_blk)
            h_in = jnp.broadcast_to(h_blk[SUBLANES - 1:SUBLANES], (SUBLANES, cols))
        hs = jnp.concatenate(outs, axis=0)
        o_ref[rows, :] = (hs * jax.nn.gelu(gy)).astype(o_ref.dtype)
        if s + 1 < n_sub:
            u, hist = conv(s + 1, hist)
    tail_ref[j] = hist
    h_ref[j] = h_in


def _rglru(hm, layer, w_y, b_y, w_x, b_x, conv_w, conv_b, w_a, b_a, w_i, b_i, lam, *, seq):
    t, d = hm.shape
    r = w_x.shape[-1]
    cols = LRU_COLS
    nblk = cols // LRU_BLOCK
    ncb = r // cols
    tm = min(1024, seq)
    vec = lambda: pl.BlockSpec((1, cols), lambda i, j: (0, j))
    proj = pl.BlockSpec((d, cols), lambda i, j: (0, j))
    wblk = pl.BlockSpec((None, nblk, LRU_BLOCK, LRU_BLOCK), lambda i, j: (layer, j, 0, 0))
    blocks = _nbytes((tm, d), BF16) + 2 * _nbytes((d, cols), BF16) + _nbytes((tm, cols), BF16) \
        + 2 * _nbytes((nblk, LRU_BLOCK, LRU_BLOCK), BF16) + 10 * _nbytes((1, cols), F32)
    scratch_bytes = 2 * _nbytes((ncb, SUBLANES, cols), F32) + 2 * LRU_SLOTS * _nbytes((LRU_TILE, cols), F32)
    return pl.pallas_call(
        functools.partial(_rglru_kernel, tiles_per_batch=seq // tm),
        out_shape=jax.ShapeDtypeStruct((t, r), BF16),
        grid=(t // tm, ncb),
        in_specs=[pl.BlockSpec((tm, d), lambda i, j: (i, 0)),
                  proj, vec(), proj, vec(),
                  pl.BlockSpec((LRU_CONV, cols), lambda i, j: (0, j)), vec(),
                  wblk, vec(), wblk, vec(), vec()],
        out_specs=pl.BlockSpec((tm, cols), lambda i, j: (i, j)),
        scratch_shapes=[pltpu.VMEM((ncb, SUBLANES, cols), F32), pltpu.VMEM((ncb, SUBLANES, cols), F32),
                        pltpu.VMEM((LRU_SLOTS, LRU_TILE, cols), F32),
                        pltpu.VMEM((LRU_SLOTS, LRU_TILE, cols), F32)],
        compiler_params=_params(("arbitrary", "arbitrary"),
                                _vmem_limit(blocks, scratch_bytes + (16 << 20))),
        name="rglru_mixer",
    )(hm, w_y, b_y.reshape(1, r), w_x, b_x.reshape(1, r), conv_w, conv_b.reshape(1, r),
      w_a, b_a.reshape(1, r), w_i, b_i.reshape(1, r), lam.reshape(1, r))


def _even_in_weights(w_in):
    d = w_in.shape[0]
    o_z, o_xbc = 0, SSD_WIDTH
    o_dt = o_xbc + SSD_WIDTH + 2 * SSD_BC_WIDTH
    o_cq = o_dt + SSD_HEADS
    q_rank = d // 4
    kv_rank = d // 8
    o_ckv = o_cq + q_rank
    o_kr = o_ckv + kv_rank
    w_main = jnp.concatenate([w_in[:, o_z:o_dt], w_in[:, o_cq:o_kr]], axis=1).astype(BF16)
    w_small = jnp.concatenate(
        [w_in[:, o_kr:o_kr + MLA_ROPE], w_in[:, o_dt:o_cq],
         jnp.zeros((d, LANES - MLA_ROPE - SSD_HEADS), w_in.dtype)], axis=1).astype(BF16)
    return w_main, w_small, q_rank, kv_rank


def _q_weights(w_uq):
    r = w_uq.shape[0]
    w = w_uq.reshape(r, MLA_HEADS, MLA_NOPE + MLA_ROPE)
    nope = w[:, :, :MLA_NOPE].reshape(r, MLA_HEADS * MLA_NOPE)
    rope = jnp.pad(w[:, :, MLA_NOPE:], ((0, 0), (0, 0), (0, LANES - MLA_ROPE)))
    return jnp.concatenate([nope, rope.reshape(r, MLA_HEADS * LANES)], axis=1).astype(BF16)


def _kv_weights(w_ukv):
    r = w_ukv.shape[0]
    w = w_ukv.reshape(r, MLA_HEADS, MLA_NOPE + MLA_V_DIM)
    k = w[:, :, :MLA_NOPE].reshape(r, MLA_HEADS * MLA_NOPE)
    v = w[:, :, MLA_NOPE:].reshape(r, MLA_HEADS * MLA_V_DIM)
    return jnp.concatenate([k, v], axis=1).astype(BF16)


def kernel(x, c, positions, ada_w, ada_b, ada_table, norm_mix, norm_ffn, norm_final, ffn_up, ffn_down, e_w_in, e_conv_w, e_conv_b, e_dt_bias, e_a_log, e_d_skip, e_ssd_norm, e_q_norm, e_w_uq, e_kv_norm, e_w_ukv, e_w_out, o_w_y, o_b_y, o_w_x, o_b_x, o_conv_w, o_conv_b, o_w_a, o_b_a, o_w_i, o_b_i, o_lam, o_w_out, o_b_out):
    bsz, seq, d = x.shape
    t = bsz * seq
    depth = ada_table.shape[0]
    xf = x.reshape(t, d)

    cc, ss = _rope_tables(positions)
    mod_all = _ada_proj(c, ada_w, ada_b).reshape(bsz, N_MOD, d)

    o_w_a_b, o_w_i_b = o_w_a.astype(BF16), o_w_i.astype(BF16)

    def mixer_casts(layer):
        jj = layer // 2
        if layer % 2 == 0:
            return {"e_out": (e_w_out, jj)}
        return {"o_y": (o_w_y, jj), "o_x": (o_w_x, jj), "o_out": (o_w_out, jj)}

    def mm_casting(xin, w, wanted, **kw):
        names = list(wanted)
        outs = _mm(xin, w, cast=[wanted[n] for n in names], **kw)
        return outs[0], dict(zip(names, outs[1:]))

    wb = {}

    for layer in range(depth):
        mod = mod_all + ada_table[layer]
        shift_m, scale_m, gate_m, shift_f, scale_f, gate_f = (mod[:, i:i + 1] for i in range(N_MOD))
        hm = _rownorm(xf, norm_mix[layer], scale=scale_m, shift=shift_m, rows_per_batch=seq)
        j = layer // 2
        if layer % 2 == 0:
            w_main, w_small, q_rank, kv_rank = _even_in_weights(e_w_in[j])
            first = {"up": (ffn_up, 0), **mixer_casts(0)} if layer == 0 else {}
            names = list(first)
            outs = _mm(hm, w_main, tn=768, aux_w=w_small, cast=[first[nm] for nm in names])
            main, krdt = outs[0], outs[1]
            wb.update(zip(names, outs[2:]))
            o_cq = 2 * SSD_WIDTH + 2 * SSD_BC_WIDTH
            q = _norm_mm(main, e_q_norm[j], _q_weights(e_w_uq[j]), col_block=o_cq // q_rank)
            kv = _norm_mm(main, e_kv_norm[j], _kv_weights(e_w_ukv[j]),
                          col_block=(o_cq + q_rank) // kv_rank)
            o_attn = _attention(q, kv, krdt, cc, ss, bsz=bsz, seq=seq)
            y_ssd = _ssd(main, krdt, e_conv_w[j], e_conv_b[j], e_dt_bias[j], e_a_log[j],
                         e_d_skip[j], e_ssd_norm[j], bsz=bsz, seq=seq)
            xf = _mm_res([(y_ssd, wb["e_out"], 0), (o_attn, wb["e_out"], 1)], xf, gate_m,
                         rows_per_batch=seq)
        else:
            y_lru = _rglru(hm, j, wb["o_y"], o_b_y[j], wb["o_x"], o_b_x[j], o_conv_w[j], o_conv_b[j],
                           o_w_a_b, o_b_a[j], o_w_i_b, o_b_i[j], o_lam[j], seq=seq)
            r = y_lru.shape[1]
            xf = _mm_res([(y_lru, wb["o_out"], 0)], xf, gate_m, bias=o_b_out[j],
                         rows_per_batch=seq, tk=r // 2)
        hf = _rownorm(xf, norm_ffn[layer], scale=scale_f, shift=shift_f, rows_per_batch=seq)
        wanted = {"down": (ffn_down, layer)}
        if layer + 1 < depth:
            wanted.update({"up": (ffn_up, layer + 1), **mixer_casts(layer + 1)})
        hid, nxt = mm_casting(hf, wb["up"], wanted, act="relu2", out_dtype=BF16)
        wb.update(nxt)
        xf = _mm_res([(hid, wb["down"], 0)], xf, gate_f, rows_per_batch=seq, tk=4096)

    out = _rownorm(xf, norm_final, out_dtype=F32)
    return out.reshape(bsz, seq, d)
```

```python
import functools
import math

import jax
import jax.numpy as jnp
from jax import lax
from jax.experimental import pallas as pl
from jax.experimental.pallas import tpu as pltpu

F32 = jnp.float32
BF16 = jnp.bfloat16

EPS = 1e-6
LOG2_E = math.log2(math.e)
ROPE_THETA = 10000.0
N_MOD = 6

SSD_HEAD_DIM = 64
SSD_GROUPS = 8
SSD_HEADS_PER_GROUP = 4
SSD_STATE = 128
SSD_CONV = 4
SSD_HEADS = SSD_GROUPS * SSD_HEADS_PER_GROUP
SSD_WIDTH = SSD_HEADS * SSD_HEAD_DIM
SSD_BC_WIDTH = SSD_GROUPS * SSD_STATE
SSD_TILE = 256

MLA_HEADS = 16
MLA_NOPE = 128
MLA_ROPE = 64
MLA_V_DIM = 128
MLA_STREAM_CHUNK = 64
MLA_Q_TILE = 512

LRU_BLOCK = 256
LRU_CONV = 4
LRU_C = 8.0
LRU_TILE = 256
LRU_COLS = 512
LRU_SLOTS = 3

LANES = 128
SUBLANES = 8
DT_LANE0 = 64
VMEM_LIMIT_CAP = 60000 * 1024


def _vmem_limit(block_bytes, extra=0):
    need = 2 * block_bytes + extra + (4 << 20)
    return int(min(max(need, 16 << 20), VMEM_LIMIT_CAP))


def _params(semantics, vmem, flags=None):
    return pltpu.CompilerParams(dimension_semantics=semantics, vmem_limit_bytes=vmem, flags=flags)


def _nbytes(shape, dtype):
    n = 1
    for s in shape:
        n *= s
    return n * jnp.dtype(dtype).itemsize


def _silu(x):
    return x * jax.nn.sigmoid(x)


def _softplus(x):
    return jnp.maximum(x, 0.0) + jnp.log(1.0 + jnp.exp(-jnp.abs(x)))


def _split3(v):
    a1 = v.astype(BF16)
    r1 = v - a1.astype(F32)
    a2 = r1.astype(BF16)
    a3 = (r1 - a2.astype(F32)).astype(BF16)
    return a1, a2, a3


def _causal_conv(hist, blk, w, bias):
    rows, cols = blk.shape
    taps = w.shape[0]
    groups = rows // SUBLANES
    x3 = jnp.concatenate([hist, blk], axis=0).reshape(groups + 1, SUBLANES, cols)
    sub = lax.broadcasted_iota(jnp.int32, (groups, SUBLANES, cols), 1)
    y = bias + w[taps - 1:taps] * x3[1:]
    for s in range(1, taps):
        r = pltpu.roll(x3, s, 1)
        y = y + w[taps - 1 - s:taps - s] * jnp.where(sub >= s, r[1:], r[:-1])
    return y.reshape(rows, cols)


def _dot(a, b):
    return jnp.dot(a, b, preferred_element_type=F32)


def _dot_nt(a, b):
    return lax.dot_general(a, b, (((1,), (1,)), ((), ())), preferred_element_type=F32)


def _dot_tn(a, b):
    return lax.dot_general(a, b, (((0,), (0,)), ((), ())), preferred_element_type=F32)


def _select_dot(sel, v):
    a1, a2, a3 = _split3(v)
    return _dot(sel, a1) + _dot(sel, a2) + _dot(sel, a3)


def _dot_select(v, sel):
    a1, a2, a3 = _split3(v)
    return _dot(a1, sel) + _dot(a2, sel) + _dot(a3, sel)


def _ada_kernel(c_ref, w_ref, b_ref, o_ref):
    s = _silu(c_ref[...]).astype(BF16)
    o_ref[...] = _dot(s, w_ref[...].astype(BF16)) + b_ref[...]


def _ada_proj(c, w, b):
    bsz, d = c.shape
    n = w.shape[1]
    tn = 512
    blocks = _nbytes((d, tn), F32) + _nbytes((bsz, d), F32) + 2 * _nbytes((bsz, tn), F32)
    return pl.pallas_call(
        _ada_kernel,
        out_shape=jax.ShapeDtypeStruct((bsz, n), F32),
        grid=(n // tn,),
        in_specs=[pl.BlockSpec((bsz, d), lambda j: (0, 0)),
                  pl.BlockSpec((d, tn), lambda j: (0, j)),
                  pl.BlockSpec((1, tn), lambda j: (0, j))],
        out_specs=pl.BlockSpec((bsz, tn), lambda j: (0, j)),
        compiler_params=_params(("arbitrary",), _vmem_limit(blocks, _nbytes((d, tn), BF16))),
        name="ada_proj",
    )(c, w, b.reshape(1, n))


def _rope_table_kernel(pos_ref, freq_ref, cc_ref, ss_ref):
    ang = pos_ref[...].astype(F32) * freq_ref[...]
    lane = lax.broadcasted_iota(jnp.int32, ang.shape, 1)
    half = MLA_ROPE // 2
    c = jnp.cos(ang)
    s = jnp.sin(ang)
    cc_ref[...] = jnp.where(lane < MLA_ROPE, c, 0.0)
    ss_ref[...] = jnp.where(lane < half, -s, jnp.where(lane < MLA_ROPE, s, 0.0))


def _rope_tables(positions):
    t = positions.size
    half = MLA_ROPE // 2
    inv_freq = ROPE_THETA ** (-jnp.arange(0, MLA_ROPE, 2, dtype=F32) / MLA_ROPE)
    freq = jnp.concatenate([inv_freq, inv_freq, jnp.zeros((LANES - 2 * half,), F32)]).reshape(1, LANES)
    tm = min(t, 2048)
    return pl.pallas_call(
        _rope_table_kernel,
        out_shape=(jax.ShapeDtypeStruct((t, LANES), F32), jax.ShapeDtypeStruct((t, LANES), F32)),
        grid=(t // tm,),
        in_specs=[pl.BlockSpec((tm, 1), lambda i: (i, 0)),
                  pl.BlockSpec((1, LANES), lambda i: (0, 0))],
        out_specs=(pl.BlockSpec((tm, LANES), lambda i: (i, 0)),
                   pl.BlockSpec((tm, LANES), lambda i: (i, 0))),
        compiler_params=_params(("arbitrary",), _vmem_limit(4 * _nbytes((tm, LANES), F32))),
        name="rope_tables",
    )(positions.reshape(t, 1), freq)


def _rownorm_kernel(x_ref, g_ref, *rest, modulate):
    o_ref = rest[-1]
    x = x_ref[...]
    y = x * lax.rsqrt(jnp.mean(x * x, axis=-1, keepdims=True) + EPS) * g_ref[...]
    if modulate:
        sc_ref, sh_ref = rest[0], rest[1]
        y = y * (1.0 + sc_ref[...]) + sh_ref[...]
    o_ref[...] = y.astype(o_ref.dtype)


def _rownorm(x, gain, *, col_block=0, width=None, scale=None, shift=None, rows_per_batch=None,
             out_dtype=BF16):
    t = x.shape[0]
    width = x.shape[1] if width is None else width
    tm = min(512, t if rows_per_batch is None else rows_per_batch)
    modulate = scale is not None
    in_specs = [pl.BlockSpec((tm, width), lambda i: (i, col_block)),
                pl.BlockSpec((1, width), lambda i: (0, 0))]
    args = [x, gain.reshape(1, width)]
    if modulate:
        bmap = lambda i: ((i * tm) // rows_per_batch, 0, 0)
        in_specs += [pl.BlockSpec((None, 1, width), bmap), pl.BlockSpec((None, 1, width), bmap)]
        args += [scale, shift]
    blocks = _nbytes((tm, width), F32) + _nbytes((tm, width), out_dtype) + 3 * _nbytes((1, width), F32)
    return pl.pallas_call(
        functools.partial(_rownorm_kernel, modulate=modulate),
        out_shape=jax.ShapeDtypeStruct((t, width), out_dtype),
        grid=(t // tm,),
        in_specs=in_specs,
        out_specs=pl.BlockSpec((tm, width), lambda i: (i, 0)),
        compiler_params=_params(("arbitrary",), _vmem_limit(blocks, 2 * _nbytes((tm, width), F32))),
        name="rownorm",
    )(*args)


def _mm_kernel(x_ref, w_ref, *rest, act, has_bias, has_aux, n_cast):
    pos = int(has_bias)
    aux_w_ref = rest[pos] if has_aux else None
    pos += int(has_aux)
    cast_in = rest[pos:pos + n_cast]
    o_ref = rest[pos + n_cast]
    aux_o_ref = rest[pos + n_cast + 1] if has_aux else None
    cast_out = rest[pos + n_cast + 1 + int(has_aux):]
    if has_aux:
        @pl.when(pl.program_id(1) == 0)
        def _():
            aux_o_ref[...] = _dot(x_ref[...], aux_w_ref[...])
    acc = _dot(x_ref[...], w_ref[...])
    if has_bias:
        acc = acc + rest[0][...]
    if act == "relu2":
        r = jnp.maximum(acc, 0.0)
        acc = r * r
    o_ref[...] = acc.astype(o_ref.dtype)
    for src, dst in zip(cast_in, cast_out):
        dst[...] = src[...].astype(dst.dtype)


def _weight_spec(w, layer, rows, cols, index):
    if w.ndim == 2:
        return pl.BlockSpec((rows, cols), index)
    return pl.BlockSpec((None, rows, cols), lambda *g: (layer,) + tuple(index(*g)))


BF16_ROWS = 16


def _cast_slab(rows, steps):
    slab = BF16_ROWS
    while rows % slab or rows // slab > steps:
        slab += BF16_ROWS
    return slab


def _mm(x, w, *, layer=None, bias=None, act=None, out_dtype=F32, tm=1024, tn=1024, cast=(),
        aux_w=None):
    m, k = x.shape
    n = w.shape[-1]
    tm = min(tm, m)
    tn = min(tn, n)
    gj = n // tn
    steps = (m // tm) * gj
    has_bias = bias is not None
    has_aux = aux_w is not None
    in_specs = [pl.BlockSpec((tm, k), lambda i, j: (i, 0)),
                _weight_spec(w, layer, k, tn, lambda i, j: (0, j))]
    args = [x, w]
    if has_bias:
        in_specs.append(pl.BlockSpec((1, tn), lambda i, j: (0, j)))
        args.append(bias.reshape(1, n))
    blocks = _nbytes((tm, k), x.dtype) + _nbytes((k, tn), w.dtype) + _nbytes((tm, tn), out_dtype)
    out_shape = [jax.ShapeDtypeStruct((m, n), out_dtype)]
    out_specs = [pl.BlockSpec((tm, tn), lambda i, j: (i, j))]
    if has_aux:
        na = aux_w.shape[1]
        in_specs.append(pl.BlockSpec((k, na), lambda i, j: (0, 0)))
        args.append(aux_w)
        out_shape.append(jax.ShapeDtypeStruct((m, na), F32))
        out_specs.append(pl.BlockSpec((tm, na), lambda i, j: (i, 0)))
        blocks += _nbytes((k, na), aux_w.dtype) + _nbytes((tm, na), F32)
    for src, src_layer in cast:
        _, rows, cols = src.shape
        slab = _cast_slab(rows, steps)
        n_slabs = rows // slab
        step = lambda i, j, n_slabs=n_slabs: jnp.minimum(i * gj + j, n_slabs - 1)
        in_specs.append(pl.BlockSpec((None, slab, cols),
                                     lambda i, j, src_layer=src_layer, step=step: (src_layer, step(i, j), 0)))
        args.append(src)
        out_shape.append(jax.ShapeDtypeStruct((rows, cols), BF16))
        out_specs.append(pl.BlockSpec((slab, cols), lambda i, j, step=step: (step(i, j), 0)))
        blocks += _nbytes((slab, cols), F32) + _nbytes((slab, cols), BF16)
    outs = pl.pallas_call(
        functools.partial(_mm_kernel, act=act, has_bias=has_bias, has_aux=has_aux, n_cast=len(cast)),
        out_shape=out_shape,
        grid=(m // tm, gj),
        in_specs=in_specs,
        out_specs=out_specs,
        compiler_params=_params(("arbitrary", "arbitrary"),
                                _vmem_limit(blocks, 3 * _nbytes((tm, tn), F32))),
        name="mm",
    )(*args)
    return outs if (cast or has_aux) else outs[0]


def _norm_mm_kernel(x_ref, g_ref, w_ref, o_ref, xn_ref):
    @pl.when(pl.program_id(1) == 0)
    def _():
        x = x_ref[...]
        y = x * lax.rsqrt(jnp.mean(x * x, axis=-1, keepdims=True) + EPS) * g_ref[...]
        xn_ref[...] = y.astype(xn_ref.dtype)

    o_ref[...] = _dot(xn_ref[...], w_ref[...])


def _norm_mm(x, gain, w, *, col_block, tm=1024, tn=1024):
    m = x.shape[0]
    k, n = w.shape
    tm = min(tm, m)
    blocks = _nbytes((tm, k), F32) + _nbytes((k, tn), w.dtype) + _nbytes((tm, tn), F32)
    return pl.pallas_call(
        _norm_mm_kernel,
        out_shape=jax.ShapeDtypeStruct((m, n), F32),
        grid=(m // tm, n // tn),
        in_specs=[pl.BlockSpec((tm, k), lambda i, j: (i, col_block)),
                  pl.BlockSpec((1, k), lambda i, j: (0, 0)),
                  pl.BlockSpec((k, tn), lambda i, j: (0, j))],
        out_specs=pl.BlockSpec((tm, tn), lambda i, j: (i, j)),
        scratch_shapes=[pltpu.VMEM((tm, k), BF16)],
        compiler_params=_params(("arbitrary", "arbitrary"),
                                _vmem_limit(blocks, _nbytes((tm, k), BF16) + 3 * _nbytes((tm, tn), F32))),
        name="norm_mm",
    )(x, gain.reshape(1, k), w)


def _mm_res_kernel(*refs, n_pairs, has_bias, nk):
    xs = refs[0:2 * n_pairs:2]
    ws = refs[1:2 * n_pairs:2]
    pos = 2 * n_pairs
    res_ref, gate_ref = refs[pos], refs[pos + 1]
    pos += 2
    bias_ref = refs[pos] if has_bias else None
    pos += int(has_bias)
    o_ref = refs[pos]

    def partial_sum():
        acc = _dot(xs[0][...], ws[0][...])
        for x_ref, w_ref in zip(xs[1:], ws[1:]):
            acc = acc + _dot(x_ref[...], w_ref[...])
        return acc

    def finish(acc):
        if has_bias:
            acc = acc + bias_ref[...]
        o_ref[...] = res_ref[...] + gate_ref[...] * acc

    if nk == 1:
        finish(partial_sum())
        return

    kk = pl.program_id(2)

    @pl.when(kk == 0)
    def _():
        o_ref[...] = partial_sum()

    @pl.when((kk > 0) & (kk < nk - 1))
    def _():
        o_ref[...] += partial_sum()

    @pl.when(kk == nk - 1)
    def _():
        finish(o_ref[...] + partial_sum())


def _mm_res(pairs, res, gate, *, layer=None, bias=None, rows_per_batch, tm=1024, tn=1024, tk=None):
    m, n = res.shape
    k = pairs[0][0].shape[1]
    tk = k if tk is None else tk
    nk = k // tk
    tm = min(tm, rows_per_batch)
    has_bias = bias is not None
    in_specs, args = [], []
    blocks = 0
    for x, w, row0 in pairs:
        in_specs += [pl.BlockSpec((tm, tk), lambda i, j, kk: (i, kk)),
                     _weight_spec(w, layer, tk, tn, lambda i, j, kk, row0=row0: (row0 + kk, j))]
        args += [x, w]
        blocks += _nbytes((tm, tk), x.dtype) + _nbytes((tk, tn), w.dtype)
    in_specs += [pl.BlockSpec((tm, tn), lambda i, j, kk: (i, j)),
                 pl.BlockSpec((None, 1, tn), lambda i, j, kk: ((i * tm) // rows_per_batch, 0, j))]
    args += [res, gate]
    blocks += 2 * _nbytes((tm, tn), F32)
    if has_bias:
        in_specs.append(pl.BlockSpec((1, tn), lambda i, j, kk: (0, j)))
        args.append(bias.reshape(1, n))
    return pl.pallas_call(
        functools.partial(_mm_res_kernel, n_pairs=len(pairs), has_bias=has_bias, nk=nk),
        out_shape=jax.ShapeDtypeStruct((m, n), F32),
        grid=(m // tm, n // tn, nk),
        in_specs=in_specs,
        out_specs=pl.BlockSpec((tm, tn), lambda i, j, kk: (i, j)),
        compiler_params=_params(("arbitrary", "arbitrary", "arbitrary"),
                                _vmem_limit(blocks, 4 * _nbytes((tm, tn), F32))),
        name="mm_res",
    )(*args)


def _attn_kernel(qn_ref, qr_ref, kn_ref, v_ref, kr_ref, cc_ref, ss_ref, o_ref, *, scale, tq):
    seq = qn_ref.shape[0]
    cc = cc_ref[...]
    ss = ss_ref[...]
    lane = lax.broadcasted_iota(jnp.int32, (seq, LANES), 1)
    half = MLA_ROPE // 2

    def rope(x):
        swapped = jnp.where(lane < half, pltpu.roll(x, LANES - half, 1), pltpu.roll(x, half, 1))
        return x * cc + swapped * ss

    qf = jnp.concatenate([(qn_ref[...] * scale).astype(BF16),
                          (rope(qr_ref[...]) * scale).astype(BF16)], axis=-1)
    kf = jnp.concatenate([kn_ref[...].astype(BF16), rope(kr_ref[...]).astype(BF16)], axis=-1)
    vt = v_ref[...].T.astype(BF16)

    k_chunk = lax.broadcasted_iota(jnp.int32, (tq, tq), 0) // MLA_STREAM_CHUNK
    q_chunk = lax.broadcasted_iota(jnp.int32, (tq, tq), 1) // MLA_STREAM_CHUNK
    diag_ok = k_chunk <= q_chunk

    for i in range(seq // tq):
        q0, q1 = i * tq, (i + 1) * tq
        s_d = jnp.where(diag_ok, _dot_nt(kf[q0:q1], qf[q0:q1]), -jnp.inf)
        m = jnp.max(s_d, axis=0, keepdims=True)
        if i > 0:
            s_o = _dot_nt(kf[:q0], qf[q0:q1])
            m = jnp.maximum(m, jnp.max(s_o, axis=0, keepdims=True))
        p_d = jnp.exp2(s_d - m)
        denom = jnp.sum(p_d, axis=0, keepdims=True)
        acc = _dot(vt[:, q0:q1], p_d.astype(BF16))
        if i > 0:
            p_o = jnp.exp2(s_o - m)
            denom = denom + jnp.sum(p_o, axis=0, keepdims=True)
            acc = acc + _dot(vt[:, :q0], p_o.astype(BF16))
        o_ref[q0:q1, :] = (acc / denom).T.astype(o_ref.dtype)


def _attention(q, kv, krdt, cc, ss, *, bsz, seq):
    t = q.shape[0]
    scale = (MLA_NOPE + MLA_ROPE) ** -0.5 * math.log2(math.e)
    blk = lambda off: pl.BlockSpec((seq, LANES), lambda b, h: (b, off + h))
    tok = pl.BlockSpec((seq, LANES), lambda b, h: (b, 0))
    blocks = 7 * _nbytes((seq, LANES), F32) + _nbytes((seq, LANES), BF16)
    return pl.pallas_call(
        functools.partial(_attn_kernel, scale=scale, tq=MLA_Q_TILE),
        out_shape=jax.ShapeDtypeStruct((t, MLA_HEADS * MLA_V_DIM), BF16),
        grid=(bsz, MLA_HEADS),
        in_specs=[blk(0), blk(MLA_HEADS), blk(0), blk(MLA_HEADS), tok, tok, tok],
        out_specs=pl.BlockSpec((seq, MLA_V_DIM), lambda b, h: (b, h)),
        compiler_params=_params(("arbitrary", "arbitrary"), _vmem_limit(blocks, 16 << 20)),
        name="mla_attention",
    )(q, q, kv, kv, krdt, cc, ss)


def _ssd_kernel(z_ref, x_ref, bc_ref, dk_ref, cwx_ref, cbx_ref, cwbc_ref, cbbc_ref,
                dtb_ref, alog_ref, dskip_ref, gain_ref, o_ref,
                tailx_ref, tailbc_ref, state_ref):
    q = x_ref.shape[0]
    p = SSD_HEAD_DIM
    gw = SSD_HEADS_PER_GROUP * p
    n = SSD_STATE

    @pl.when(pl.program_id(1) == 0)
    def _():
        tailx_ref[...] = jnp.zeros_like(tailx_ref)
        tailbc_ref[...] = jnp.zeros_like(tailbc_ref)
        state_ref[...] = jnp.zeros_like(state_ref)

    def conv_silu(tail_ref, blk_ref, w_ref, b_ref):
        blk = blk_ref[...]
        y = _causal_conv(tail_ref[...], blk, w_ref[...], b_ref[...])
        tail_ref[...] = blk[q - SUBLANES:]
        return _silu(y)

    xc = conv_silu(tailx_ref, x_ref, cwx_ref, cbx_ref)
    bcc = conv_silu(tailbc_ref, bc_ref, cwbc_ref, cbbc_ref)

    lane = lax.broadcasted_iota(jnp.int32, (q, LANES), 1)
    lane1 = lax.broadcasted_iota(jnp.int32, (1, LANES), 1)
    is_dt = (lane1 >= DT_LANE0) & (lane1 < DT_LANE0 + SSD_HEADS)
    dt = _softplus(dk_ref[...] + dtb_ref[...])
    a_row = jnp.where(is_dt, -jnp.exp(alog_ref[...]) * LOG2_E, 0.0)
    ad = dt * a_row

    rr = lax.broadcasted_iota(jnp.int32, (q, q), 0)
    cc_ = lax.broadcasted_iota(jnp.int32, (q, q), 1)
    causal = cc_ <= rr
    tri = jnp.where(causal, 1.0, 0.0).astype(BF16)
    a_cs = _select_dot(tri, ad)
    a_cs_t = a_cs.T

    er = lax.broadcasted_iota(jnp.int32, (LANES, SSD_WIDTH), 0)
    ec = lax.broadcasted_iota(jnp.int32, (LANES, SSD_WIDTH), 1)
    expand = jnp.where(er - DT_LANE0 == ec // p, 1.0, 0.0).astype(BF16)
    dt_x = _dot_select(dt, expand)
    acs_x = _dot_select(a_cs, expand)
    dec_x = jnp.exp2(acs_x)
    aend_x = acs_x[q - 1:q, :]
    xs = xc * dt_x
    xs_end = xs * jnp.exp2(aend_x - acs_x)
    chunk_decay = dec_x[q - 1:q, :]

    lo_half = lane < p
    zs = _silu(z_ref[...])
    dskip = dskip_ref[...]
    gain = gain_ref[...]

    for g in range(SSD_GROUPS):
        gs = slice(g * gw, (g + 1) * gw)
        b_g = bcc[:, g * n:(g + 1) * n].astype(BF16)
        c_g = bcc[:, SSD_BC_WIDTH + g * n:SSD_BC_WIDTH + (g + 1) * n].astype(BF16)
        cb = _dot_nt(c_g, b_g)
        s_prev = state_ref[g]
        y_off = _dot(c_g, s_prev.astype(BF16)) * dec_x[:, gs]
        s_new = _dot_tn(b_g, xs_end[:, gs].astype(BF16))
        state_ref[g] = chunk_decay[:, gs] * s_prev + s_new

        pair_out = []
        for pr in range(SSD_HEADS_PER_GROUP // 2):
            ps = slice(g * gw + pr * LANES, g * gw + (pr + 1) * LANES)
            xs_pair = xs[:, ps]
            acc = None
            for hh in range(2):
                h = g * SSD_HEADS_PER_GROUP + pr * 2 + hh
                col = jnp.sum(jnp.where(lane == DT_LANE0 + h, a_cs, 0.0), axis=-1, keepdims=True)
                row = a_cs_t[DT_LANE0 + h:DT_LANE0 + h + 1, :]
                seg = jnp.exp2(jnp.where(causal, col - row, -jnp.inf))
                mh = (cb * seg).astype(BF16)
                keep = lo_half if hh == 0 else jnp.logical_not(lo_half)
                term = _dot(mh, jnp.where(keep, xs_pair, 0.0).astype(BF16))
                acc = term if acc is None else acc + term
            pair_out.append(acc)
        y = jnp.concatenate(pair_out, axis=-1) + y_off + dskip[:, gs] * xc[:, gs]
        y = y * zs[:, gs]
        y = y * lax.rsqrt(jnp.mean(y * y, axis=-1, keepdims=True) + EPS) * gain[:, gs]
        o_ref[:, gs] = y.astype(o_ref.dtype)


def _ssd(main, krdt, conv_w, conv_b, dt_bias, a_log, d_skip, ssd_norm, *, bsz, seq):
    t = main.shape[0]
    q = SSD_TILE
    steps = seq // q
    w = SSD_WIDTH
    row = lambda off: pl.BlockSpec((q, w), lambda b, c: (b * steps + c, off))
    full = lambda r: pl.BlockSpec((r, w), lambda b, c: (0, 0))
    lane_row = pl.BlockSpec((1, LANES), lambda b, c: (0, 0))
    pad = lambda v: jnp.zeros((1, LANES), F32).at[0, DT_LANE0:DT_LANE0 + SSD_HEADS].set(v)
    blocks = 3 * _nbytes((q, w), F32) + _nbytes((q, LANES), F32) + _nbytes((q, w), BF16) \
        + 12 * _nbytes((1, w), F32)
    scratch_bytes = 2 * _nbytes((SUBLANES, w), F32) + _nbytes((SSD_GROUPS, SSD_STATE, w // SSD_GROUPS), F32)
    return pl.pallas_call(
        _ssd_kernel,
        out_shape=jax.ShapeDtypeStruct((t, w), BF16),
        grid=(bsz, steps),
        in_specs=[row(0), row(1), row(2),
                  pl.BlockSpec((q, LANES), lambda b, c: (b * steps + c, 0)),
                  full(SSD_CONV), full(1), full(SSD_CONV), full(1),
                  lane_row, lane_row, full(1), full(1)],
        out_specs=pl.BlockSpec((q, w), lambda b, c: (b * steps + c, 0)),
        scratch_shapes=[pltpu.VMEM((SUBLANES, w), F32), pltpu.VMEM((SUBLANES, w), F32),
                        pltpu.VMEM((SSD_GROUPS, SSD_STATE, w // SSD_GROUPS), F32)],
        compiler_params=_params(("arbitrary", "arbitrary"),
                                _vmem_limit(blocks, scratch_bytes + (24 << 20))),
        name="ssd_mixer",
    )(main, main, main, krdt,
      conv_w[:, :w], conv_b[:w].reshape(1, w), conv_w[:, w:], conv_b[w:].reshape(1, w),
      pad(dt_bias), pad(a_log),
      jnp.repeat(d_skip, SSD_HEAD_DIM).reshape(1, w), ssd_norm.reshape(1, w))


def _rglru_kernel(x_ref, wy_ref, by_ref, wx_ref, bx_ref, cw_ref, cb_ref, wa_ref, ba_ref,
                  wi_ref, bi_ref, lam_ref, o_ref, tail_ref, h_ref, ux_buf, gy_buf, *, tiles_per_batch):
    i = pl.program_id(0)
    j = pl.program_id(1)
    tm, cols = o_ref.shape
    tl = LRU_TILE
    groups = tl // SUBLANES
    nblk = cols // LRU_BLOCK
    w = cw_ref[...]
    cb = cb_ref[...]
    ba = ba_ref[...]
    bi = bi_ref[...]
    by = by_ref[...]
    bx = bx_ref[...]
    neg_c_sp = -LRU_C * LOG2_E * _softplus(-lam_ref[...])
    sub = lax.broadcasted_iota(jnp.int32, (groups, SUBLANES, cols), 1)

    @pl.when(i % tiles_per_batch == 0)
    def _():
        tail_ref[j] = jnp.zeros((SUBLANES, cols), F32)
        h_ref[j] = jnp.zeros((SUBLANES, cols), F32)

    hist = tail_ref[j]
    h_in = h_ref[j]
    def project(s):
        xs = x_ref[s * tl:(s + 1) * tl, :]
        ux_buf[s % LRU_SLOTS] = _dot(xs, wx_ref[...]) + bx
        gy_buf[s % LRU_SLOTS] = _dot(xs, wy_ref[...]) + by

    def conv(s, hist):
        blk = ux_buf[s % LRU_SLOTS]
        return _causal_conv(hist, blk, w, cb), blk[tl - SUBLANES:]

    def gate_logits(u):
        ub = u.astype(BF16)
        ra, ia = [], []
        for bk in range(nblk):
            cs = slice(bk * LRU_BLOCK, (bk + 1) * LRU_BLOCK)
            ra.append(_dot(ub[:, cs], wa_ref[bk]))
            ia.append(_dot(ub[:, cs], wi_ref[bk]))
        return jnp.concatenate(ra, axis=-1) + ba, jnp.concatenate(ia, axis=-1) + bi

    n_sub = tm // tl
    for s in range(min(LRU_SLOTS - 1, n_sub)):
        project(s)
    u, hist = conv(0, hist)
    for s in range(n_sub):
        rows = slice(s * tl, (s + 1) * tl)
        r_logit, i_logit = gate_logits(u)
        if s + LRU_SLOTS - 1 < n_sub:
            project(s + LRU_SLOTS - 1)
        gy = gy_buf[s % LRU_SLOTS]
        r_gate = jax.nn.sigmoid(r_logit)
        i_gate = jax.nn.sigmoid(i_logit)
        a2 = jnp.exp2(neg_c_sp * r_gate)
        b2 = jnp.sqrt(1.0 - a2 * a2) * (i_gate * u)

        a = a2.reshape(groups, SUBLANES, cols)
        b = b2.reshape(groups, SUBLANES, cols)
        for d in (1, 2, 4):
            ok = sub >= d
            a_sh = jnp.where(ok, pltpu.roll(a, d, 1), 1.0)
            b_sh = jnp.where(ok, pltpu.roll(b, d, 1), 0.0)
            b = b + a * b_sh
            a = a * a_sh

        outs = []
        for r in range(groups):
            h_blk = b[r] + a[r] * h_in
            outs.append(h_blk)
            h_in = jnp.broadcast_to(h_blk[SUBLANES - 1:SUBLANES], (SUBLANES, cols))
        hs = jnp.concatenate(outs, axis=0)
        o_ref[rows, :] = (hs * jax.nn.gelu(gy)).astype(o_ref.dtype)
        if s + 1 < n_sub:
            u, hist = conv(s + 1, hist)
    tail_ref[j] = hist
    h_ref[j] = h_in


def _rglru(hm, layer, w_y, b_y, w_x, b_x, conv_w, conv_b, w_a, b_a, w_i, b_i, lam, *, seq):
    t, d = hm.shape
    r = w_x.shape[-1]
    cols = LRU_COLS
    nblk = cols // LRU_BLOCK
    ncb = r // cols
    tm = min(1024, seq)
    vec = lambda: pl.BlockSpec((1, cols), lambda i, j: (0, j))
    proj = pl.BlockSpec((d, cols), lambda i, j: (0, j))
    wblk = pl.BlockSpec((None, nblk, LRU_BLOCK, LRU_BLOCK), lambda i, j: (layer, j, 0, 0))
    blocks = _nbytes((tm, d), BF16) + 2 * _nbytes((d, cols), BF16) + _nbytes((tm, cols), BF16) \
        + 2 * _nbytes((nblk, LRU_BLOCK, LRU_BLOCK), BF16) + 10 * _nbytes((1, cols), F32)
    scratch_bytes = 2 * _nbytes((ncb, SUBLANES, cols), F32) + 2 * LRU_SLOTS * _nbytes((LRU_TILE, cols), F32)
    return pl.pallas_call(
        functools.partial(_rglru_kernel, tiles_per_batch=seq // tm),
        out_shape=jax.ShapeDtypeStruct((t, r), BF16),
        grid=(t // tm, ncb),
        in_specs=[pl.BlockSpec((tm, d), lambda i, j: (i, 0)),
                  proj, vec(), proj, vec(),
                  pl.BlockSpec((LRU_CONV, cols), lambda i, j: (0, j)), vec(),
                  wblk, vec(), wblk, vec(), vec()],
        out_specs=pl.BlockSpec((tm, cols), lambda i, j: (i, j)),
        scratch_shapes=[pltpu.VMEM((ncb, SUBLANES, cols), F32), pltpu.VMEM((ncb, SUBLANES, cols), F32),
                        pltpu.VMEM((LRU_SLOTS, LRU_TILE, cols), F32),
                        pltpu.VMEM((LRU_SLOTS, LRU_TILE, cols), F32)],
        compiler_params=_params(("arbitrary", "arbitrary"),
                                _vmem_limit(blocks, scratch_bytes + (16 << 20))),
        name="rglru_mixer",
    )(hm, w_y, b_y.reshape(1, r), w_x, b_x.reshape(1, r), conv_w, conv_b.reshape(1, r),
      w_a, b_a.reshape(1, r), w_i, b_i.reshape(1, r), lam.reshape(1, r))


def _even_in_weights(w_in):
    d = w_in.shape[0]
    o_z, o_xbc = 0, SSD_WIDTH
    o_dt = o_xbc + SSD_WIDTH + 2 * SSD_BC_WIDTH
    o_cq = o_dt + SSD_HEADS
    q_rank = d // 4
    kv_rank = d // 8
    o_ckv = o_cq + q_rank
    o_kr = o_ckv + kv_rank
    w_main = jnp.concatenate([w_in[:, o_z:o_dt], w_in[:, o_cq:o_kr]], axis=1).astype(BF16)
    w_small = jnp.concatenate(
        [w_in[:, o_kr:o_kr + MLA_ROPE], w_in[:, o_dt:o_cq],
         jnp.zeros((d, LANES - MLA_ROPE - SSD_HEADS), w_in.dtype)], axis=1).astype(BF16)
    return w_main, w_small, q_rank, kv_rank


def _q_weights(w_uq):
    r = w_uq.shape[0]
    w = w_uq.reshape(r, MLA_HEADS, MLA_NOPE + MLA_ROPE)
    nope = w[:, :, :MLA_NOPE].reshape(r, MLA_HEADS * MLA_NOPE)
    rope = jnp.pad(w[:, :, MLA_NOPE:], ((0, 0), (0, 0), (0, LANES - MLA_ROPE)))
    return jnp.concatenate([nope, rope.reshape(r, MLA_HEADS * LANES)], axis=1).astype(BF16)


def _kv_weights(w_ukv):
    r = w_ukv.shape[0]
    w = w_ukv.reshape(r, MLA_HEADS, MLA_NOPE + MLA_V_DIM)
    k = w[:, :, :MLA_NOPE].reshape(r, MLA_HEADS * MLA_NOPE)
    v = w[:, :, MLA_NOPE:].reshape(r, MLA_HEADS * MLA_V_DIM)
    return jnp.concatenate([k, v], axis=1).astype(BF16)


def kernel(x, c, positions, ada_w, ada_b, ada_table, norm_mix, norm_ffn, norm_final, ffn_up, ffn_down, e_w_in, e_conv_w, e_conv_b, e_dt_bias, e_a_log, e_d_skip, e_ssd_norm, e_q_norm, e_w_uq, e_kv_norm, e_w_ukv, e_w_out, o_w_y, o_b_y, o_w_x, o_b_x, o_conv_w, o_conv_b, o_w_a, o_b_a, o_w_i, o_b_i, o_lam, o_w_out, o_b_out):
    bsz, seq, d = x.shape
    t = bsz * seq
    depth = ada_table.shape[0]
    xf = x.reshape(t, d)

    cc, ss = _rope_tables(positions)
    mod_all = _ada_proj(c, ada_w, ada_b).reshape(bsz, N_MOD, d)

    o_w_a_b, o_w_i_b = o_w_a.astype(BF16), o_w_i.astype(BF16)

    def mixer_casts(layer):
        jj = layer // 2
        if layer % 2 == 0:
            return {"e_out": (e_w_out, jj)}
        return {"o_y": (o_w_y, jj), "o_x": (o_w_x, jj), "o_out": (o_w_out, jj)}

    def mm_casting(xin, w, wanted, **kw):
        names = list(wanted)
        outs = _mm(xin, w, cast=[wanted[n] for n in names], **kw)
        return outs[0], dict(zip(names, outs[1:]))

    wb = {}

    for layer in range(depth):
        mod = mod_all + ada_table[layer]
        shift_m, scale_m, gate_m, shift_f, scale_f, gate_f = (mod[:, i:i + 1] for i in range(N_MOD))
        hm = _rownorm(xf, norm_mix[layer], scale=scale_m, shift=shift_m, rows_per_batch=seq)
        j = layer // 2
        if layer % 2 == 0:
            w_main, w_small, q_rank, kv_rank = _even_in_weights(e_w_in[j])
            first = {"up": (ffn_up, 0), **mixer_casts(0)} if layer == 0 else {}
            names = list(first)
            outs = _mm(hm, w_main, tn=768, aux_w=w_small, cast=[first[nm] for nm in names])
            main, krdt = outs[0], outs[1]
            wb.update(zip(names, outs[2:]))
            o_cq = 2 * SSD_WIDTH + 2 * SSD_BC_WIDTH
            q = _norm_mm(main, e_q_norm[j], _q_weights(e_w_uq[j]), col_block=o_cq // q_rank)
            kv = _norm_mm(main, e_kv_norm[j], _kv_weights(e_w_ukv[j]),
                          col_block=(o_cq + q_rank) // kv_rank)
            o_attn = _attention(q, kv, krdt, cc, ss, bsz=bsz, seq=seq)
            y_ssd = _ssd(main, krdt, e_conv_w[j], e_conv_b[j], e_dt_bias[j], e_a_log[j],
                         e_d_skip[j], e_ssd_norm[j], bsz=bsz, seq=seq)
            xf = _mm_res([(y_ssd, wb["e_out"], 0), (o_attn, wb["e_out"], 1)], xf, gate_m,
                         rows_per_batch=seq)
        else:
            y_lru = _rglru(hm, j, wb["o_y"], o_b_y[j], wb["o_x"], o_b_x[j], o_conv_w[j], o_conv_b[j],
                           o_w_a_b, o_b_a[j], o_w_i_b, o_b_i[j], o_lam[j], seq=seq)
            r = y_lru.shape[1]
            xf = _mm_res([(y_lru, wb["o_out"], 0)], xf, gate_m, bias=o_b_out[j],
                         rows_per_batch=seq, tk=r // 2)
        hf = _rownorm(xf, norm_ffn[layer], scale=scale_f, shift=shift_f, rows_per_batch=seq)
        wanted = {"down": (ffn_down, layer)}
        if layer + 1 < depth:
            wanted.update({"up": (ffn_up, layer + 1), **mixer_casts(layer + 1)})
        hid, nxt = mm_casting(hf, wb["up"], wanted, act="relu2", out_dtype=BF16)
        wb.update(nxt)
        xf = _mm_res([(hid, wb["down"], 0)], xf, gate_f, rows_per_batch=seq, tk=4096)

    out = _rownorm(xf, norm_final, out_dtype=F32)
    return out.reshape(bsz, seq, d)
```
